```python
import math
import jax, jax.numpy as jnp
from jax import lax
import numpy as np


D_MODEL = 1024
BATCH = 16
SEQ = 4096
DEPTH = 1

MIX_WIDTH = D_MODEL
SSD_WIDTH = MIX_WIDTH // 2
SSD_HEAD_DIM = 64
SSD_HEADS = SSD_WIDTH // SSD_HEAD_DIM
SSD_GROUPS = 2
SSD_STATE = 128
SSD_CONV = 4
SSD_CHUNK = 128
SSD_BC = SSD_GROUPS * SSD_STATE
CONV_CH = SSD_WIDTH + 2 * SSD_BC
S5_WIDTH = MIX_WIDTH - SSD_WIDTH
S5_GROUP_CH = 16
S5_GROUPS = S5_WIDTH // S5_GROUP_CH
S5_STATE = 64
IN_COLS = SSD_WIDTH + CONV_CH + SSD_HEADS + S5_WIDTH
D_FF = ((8 * D_MODEL // 3 + 255) // 256) * 256
N_MOD = 9
ALPHA = (2 * DEPTH) ** 0.25
BETA = (8 * DEPTH) ** -0.25
LN_EPS = 1e-5

kernel_name = 'hymba_ssd_s5_macaron_deepnorm'


def layer_norm(x, g, b):
    xf = x.astype(jnp.float32)
    mu = jnp.mean(xf, axis=-1, keepdims=True)
    var = jnp.mean(jnp.square(xf - mu), axis=-1, keepdims=True)
    return ((xf - mu) * lax.rsqrt(var + LN_EPS) * g + b).astype(x.dtype)


def modulate(x, shift, scale):
    return x * (1 + scale[:, None, :]) + shift[:, None, :]


def swiglu(u, w1, w3, w2):
    return (jax.nn.silu(u @ w1) * (u @ w3)) @ w2


def causal_dwconv(x, w, b):
    k = w.shape[0]
    y = lax.conv_general_dilated(x, w[:, None, :], window_strides=(1,), padding=[(k - 1, 0)],
                                 dimension_numbers=('NWC', 'WIO', 'NWC'),
                                 feature_group_count=x.shape[-1])
    return y + b


def ssd_chunked(xs, dt, a, bm, cm):
    bsz, s_len, n_h, p = xs.shape
    n_g, n_s = bm.shape[-2:]
    n_z = n_h // n_g
    l = SSD_CHUNK
    nc = s_len // l
    x = (xs * dt[..., None]).reshape(bsz, nc, l, n_g, n_z, p)
    a_dt = (dt * a).reshape(bsz, nc, l, n_g, n_z).transpose(0, 3, 4, 1, 2)
    bm = bm.reshape(bsz, nc, l, n_g, n_s)
    cm = cm.reshape(bsz, nc, l, n_g, n_s)
    a_cs = jnp.cumsum(a_dt, axis=-1)
    causal = jnp.tril(jnp.ones((l, l), dtype=bool))
    seg = a_cs[..., :, None] - a_cs[..., None, :]
    lmat = jnp.exp(jnp.where(causal, seg, -jnp.inf))
    cb = jnp.einsum('bclgn,bcsgn->bcgls', cm, bm)
    y_diag = jnp.einsum('bcgls,bgzcls,bcsgzp->bclgzp', cb, lmat, x)
    decay = jnp.exp(a_cs[..., -1:] - a_cs)
    states = jnp.einsum('bclgn,bgzcl,bclgzp->bcgzpn', bm, decay, x)
    chunk_decay = jnp.exp(a_cs[..., -1])

    def step(h, inp):
        s_c, d_c = inp
        return d_c[..., None, None] * h + s_c, h

    h0 = jnp.zeros((bsz, n_g, n_z, p, n_s), dtype=states.dtype)
    _, prev = lax.scan(step, h0, (jnp.moveaxis(states, 1, 0), jnp.moveaxis(chunk_decay, 3, 0)))
    prev = jnp.moveaxis(prev, 0, 1)
    y_off = jnp.einsum('bclgn,bcgzpn,bgzcl->bclgzp', cm, prev, jnp.exp(a_cs))
    return (y_diag + y_off).reshape(bsz, s_len, n_h, p)


def s5_mixer(u, a_re, a_im, log_dt, b_re, b_im, c_re, c_im, d, w_glu, b_glu):
    f32 = jnp.float32
    bsz, s_len, _ = u.shape
    uf = u.astype(f32).reshape(bsz, s_len, S5_GROUPS, S5_GROUP_CH)
    ar, ai = a_re.astype(f32), a_im.astype(f32)
    dt = jnp.exp(log_dt.astype(f32))[:, None]
    mag = jnp.exp(dt * ar)
    ab_re, ab_im = mag * jnp.cos(dt * ai), mag * jnp.sin(dt * ai)
    den = ar * ar + ai * ai
    nr, ni = ab_re - 1.0, ab_im
    f_re, f_im = (nr * ar + ni * ai) / den, (ni * ar - nr * ai) / den
    br, bi = b_re.astype(f32), b_im.astype(f32)
    bb_re = f_re[..., None] * br - f_im[..., None] * bi
    bb_im = f_re[..., None] * bi + f_im[..., None] * br
    bu_re = jnp.einsum('bsgh,gph->bsgp', uf, bb_re)
    bu_im = jnp.einsum('bsgh,gph->bsgp', uf, bb_im)
    a_seq_re = jnp.broadcast_to(ab_re, (1, s_len, S5_GROUPS, S5_STATE))
    a_seq_im = jnp.broadcast_to(ab_im, (1, s_len, S5_GROUPS, S5_STATE))

    def combine(e1, e2):
        a1r, a1i, b1r, b1i = e1
        a2r, a2i, b2r, b2i = e2
        return (a2r * a1r - a2i * a1i, a2r * a1i + a2i * a1r,
                a2r * b1r - a2i * b1i + b2r, a2r * b1i + a2i * b1r + b2i)

    _, _, xr, xi = lax.associative_scan(combine, (a_seq_re, a_seq_im, bu_re, bu_im), axis=1)
    y = (jnp.einsum('bsgp,ghp->bsgh', xr, c_re.astype(f32))
         - jnp.einsum('bsgp,ghp->bsgh', xi, c_im.astype(f32))
         + uf * d.astype(f32).reshape(S5_GROUPS, S5_GROUP_CH))
    y = y.reshape(bsz, s_len, S5_WIDTH)
    g = jax.nn.gelu(y)
    out = g * jax.nn.sigmoid(g @ w_glu.astype(f32) + b_glu.astype(f32))
    return out.astype(u.dtype)


def hybrid_mixer(h, w_in, conv_w, conv_b, dt_bias, a_log, d_ssd, ssd_norm_w,
                 s5_a_re, s5_a_im, s5_log_dt, s5_b_re, s5_b_im, s5_c_re, s5_c_im, s5_d,
                 w_glu, b_glu, w_out):
    f32 = jnp.float32
    bsz, s_len, _ = h.shape
    proj = h @ w_in
    z, xbc, dt_raw, u = jnp.split(proj, [SSD_WIDTH, SSD_WIDTH + CONV_CH,
                                         SSD_WIDTH + CONV_CH + SSD_HEADS], axis=-1)
    xbc = jax.nn.silu(causal_dwconv(xbc, conv_w, conv_b))
    xs, bm, cm = jnp.split(xbc.astype(f32), [SSD_WIDTH, SSD_WIDTH + SSD_BC], axis=-1)
    dt = jax.nn.softplus(dt_raw.astype(f32) + dt_bias.astype(f32))
    a = -jnp.exp(a_log.astype(f32))
    xs = xs.reshape(bsz, s_len, SSD_HEADS, SSD_HEAD_DIM)
    y = ssd_chunked(xs, dt, a,
                    bm.reshape(bsz, s_len, SSD_GROUPS, SSD_STATE),
                    cm.reshape(bsz, s_len, SSD_GROUPS, SSD_STATE))
    y = y + d_ssd.astype(f32)[:, None] * xs
    y = y.reshape(bsz, s_len, SSD_WIDTH) * jax.nn.silu(z.astype(f32))
    yg = y.reshape(bsz, s_len, SSD_GROUPS, SSD_WIDTH // SSD_GROUPS)
    yg = yg * lax.rsqrt(jnp.mean(jnp.square(yg), axis=-1, keepdims=True) + LN_EPS)
    y_ssd = (yg.reshape(bsz, s_len, SSD_WIDTH) * ssd_norm_w.astype(f32)).astype(h.dtype)
    y_s5 = s5_mixer(u, s5_a_re, s5_a_im, s5_log_dt, s5_b_re, s5_b_im, s5_c_re, s5_c_im,
                    s5_d, w_glu, b_glu)
    return jnp.concatenate([y_ssd, y_s5], axis=-1) @ w_out


def setup_inputs(seed: int = 0) -> dict:
    key = jax.random.key(seed)
    ks = iter(jax.random.split(key, 48))
    f32 = jnp.float32
    nl = DEPTH

    def nrm(shape, std):
        return std * jax.random.normal(next(ks), shape, f32)

    def unif(shape, lo, hi):
        return jax.random.uniform(next(ks), shape, f32, minval=lo, maxval=hi)

    x = nrm((BATCH, SEQ, D_MODEL), 1.0)
    c = nrm((BATCH, D_MODEL), 1.0)
    w_ada = nrm((nl, D_MODEL, N_MOD * D_MODEL), 0.5 * D_MODEL ** -0.5)
    b_ada = nrm((nl, N_MOD * D_MODEL), 0.02)
    ffn1_w1 = nrm((nl, D_MODEL, D_FF), D_MODEL ** -0.5)
    ffn1_w3 = nrm((nl, D_MODEL, D_FF), D_MODEL ** -0.5)
    ffn1_w2 = nrm((nl, D_FF, D_MODEL), BETA * D_FF ** -0.5)
    ln1_g = 1.0 + nrm((nl, D_MODEL), 0.02)
    ln1_b = nrm((nl, D_MODEL), 0.02)
    w_in = nrm((nl, D_MODEL, IN_COLS), D_MODEL ** -0.5)
    conv_w = nrm((nl, SSD_CONV, CONV_CH), SSD_CONV ** -0.5)
    conv_b = nrm((nl, CONV_CH), 0.01)
    dt0 = jnp.exp(unif((nl, SSD_HEADS), math.log(1e-3), math.log(1e-1)))
    dt_bias = dt0 + jnp.log(-jnp.expm1(-dt0))
    a_log = jnp.log(unif((nl, SSD_HEADS), 1.0, 16.0))
    d_ssd = 1.0 + nrm((nl, SSD_HEADS), 0.1)
    ssd_norm_w = 1.0 + nrm((nl, SSD_WIDTH), 0.02)
    s5_a_re = -0.5 + nrm((nl, S5_GROUPS, S5_STATE), 0.01)
    s5_a_im = math.pi * jnp.arange(S5_STATE, dtype=f32)[None, None, :] + nrm((nl, S5_GROUPS, S5_STATE), 0.01)
    s5_log_dt = unif((nl, S5_GROUPS), math.log(1e-3), math.log(1e-1))
    s5_b_re = nrm((nl, S5_GROUPS, S5_STATE, S5_GROUP_CH), (2 * S5_GROUP_CH) ** -0.5)
    s5_b_im = nrm((nl, S5_GROUPS, S5_STATE, S5_GROUP_CH), (2 * S5_GROUP_CH) ** -0.5)
    s5_c_re = nrm((nl, S5_GROUPS, S5_GROUP_CH, S5_STATE), S5_STATE ** -0.5)
    s5_c_im = nrm((nl, S5_GROUPS, S5_GROUP_CH, S5_STATE), S5_STATE ** -0.5)
    s5_d = nrm((nl, S5_WIDTH), 1.0)
    w_glu = nrm((nl, S5_WIDTH, S5_WIDTH), S5_WIDTH ** -0.5)
    b_glu = nrm((nl, S5_WIDTH), 0.01)
    w_out = nrm((nl, MIX_WIDTH, D_MODEL), BETA * MIX_WIDTH ** -0.5)
    ln2_g = 1.0 + nrm((nl, D_MODEL), 0.02)
    ln2_b = nrm((nl, D_MODEL), 0.02)
    ffn2_w1 = nrm((nl, D_MODEL, D_FF), D_MODEL ** -0.5)
    ffn2_w3 = nrm((nl, D_MODEL, D_FF), D_MODEL ** -0.5)
    ffn2_w2 = nrm((nl, D_FF, D_MODEL), BETA * D_FF ** -0.5)
    ln3_g = 1.0 + nrm((nl, D_MODEL), 0.02)
    ln3_b = nrm((nl, D_MODEL), 0.02)
    return {'x': x, 'c': c, 'w_ada': w_ada, 'b_ada': b_ada,
            'ffn1_w1': ffn1_w1, 'ffn1_w3': ffn1_w3, 'ffn1_w2': ffn1_w2, 'ln1_g': ln1_g, 'ln1_b': ln1_b,
            'w_in': w_in, 'conv_w': conv_w, 'conv_b': conv_b, 'dt_bias': dt_bias, 'a_log': a_log,
            'd_ssd': d_ssd, 'ssd_norm_w': ssd_norm_w, 's5_a_re': s5_a_re, 's5_a_im': s5_a_im,
            's5_log_dt': s5_log_dt, 's5_b_re': s5_b_re, 's5_b_im': s5_b_im, 's5_c_re': s5_c_re,
            's5_c_im': s5_c_im, 's5_d': s5_d, 'w_glu': w_glu, 'b_glu': b_glu, 'w_out': w_out,
            'ln2_g': ln2_g, 'ln2_b': ln2_b, 'ffn2_w1': ffn2_w1, 'ffn2_w3': ffn2_w3, 'ffn2_w2': ffn2_w2,
            'ln3_g': ln3_g, 'ln3_b': ln3_b}


def reference(x, c, w_ada, b_ada, ffn1_w1, ffn1_w3, ffn1_w2, ln1_g, ln1_b,
              w_in, conv_w, conv_b, dt_bias, a_log, d_ssd, ssd_norm_w, s5_a_re, s5_a_im,
              s5_log_dt, s5_b_re, s5_b_im, s5_c_re, s5_c_im, s5_d, w_glu, b_glu, w_out,
              ln2_g, ln2_b, ffn2_w1, ffn2_w3, ffn2_w2, ln3_g, ln3_b):
    bsz = x.shape[0]
    cs = jax.nn.silu(c)
    for l in range(DEPTH):
        mod = (cs @ w_ada[l] + b_ada[l]).reshape(bsz, N_MOD, D_MODEL)
        sh1, sc1, g1 = mod[:, 0], mod[:, 1], mod[:, 2]
        sh2, sc2, g2 = mod[:, 3], mod[:, 4], mod[:, 5]
        sh3, sc3, g3 = mod[:, 6], mod[:, 7], mod[:, 8]
        h = modulate(x, sh1, sc1)
        x = layer_norm(ALPHA * x + 0.5 * g1[:, None, :] * swiglu(h, ffn1_w1[l], ffn1_w3[l], ffn1_w2[l]),
                       ln1_g[l], ln1_b[l])
        h = modulate(x, sh2, sc2)
        m = hybrid_mixer(h, w_in[l], conv_w[l], conv_b[l], dt_bias[l], a_log[l], d_ssd[l],
                         ssd_norm_w[l], s5_a_re[l], s5_a_im[l], s5_log_dt[l], s5_b_re[l],
                         s5_b_im[l], s5_c_re[l], s5_c_im[l], s5_d[l], w_glu[l], b_glu[l], w_out[l])
        x = layer_norm(ALPHA * x + g2[:, None, :] * m, ln2_g[l], ln2_b[l])
        h = modulate(x, sh3, sc3)
        x = layer_norm(ALPHA * x + 0.5 * g3[:, None, :] * swiglu(h, ffn2_w1[l], ffn2_w3[l], ffn2_w2[l]),
                       ln3_g[l], ln3_b[l])
    return x
```

```python
import functools
import math

import jax
import jax.numpy as jnp
from jax import lax
from jax.experimental import pallas as pl
from jax.experimental.pallas import tpu as pltpu

F32 = jnp.float32
BF16 = jnp.bfloat16
HIGHEST = lax.Precision.HIGHEST

LN_EPS = 1e-5
SUBLANES = 8
LANES = 128
VMEM_LIMIT_BYTES = 56 * 1024 * 1024

SSD_HEAD_DIM = 64
SSD_GROUPS = 2
SSD_STATE = 128
SSD_CONV = 4
SSD_CHUNK = 128
S5_GROUP_CH = 16
S5_STATE = 64
N_MOD = 9

NB = SUBLANES
FFN_TT = 64
S5_SUB_T = 32


def _dot(a, b):
    return jnp.dot(a, b, preferred_element_type=F32)


def _dot_exact(a, b):
    return jnp.dot(a, b, preferred_element_type=F32, precision=HIGHEST)


def _sigmoid(x):
    return 1.0 / (1.0 + jnp.exp(-x))


def _silu(x):
    return x * _sigmoid(x)


def _softplus(x):
    return jnp.maximum(x, 0.0) + jnp.log1p(jnp.exp(-jnp.abs(x)))


def _layer_norm(y, g, b):
    mu = jnp.mean(y, axis=-1, keepdims=True)
    yc = y - mu
    var = jnp.mean(yc * yc, axis=-1, keepdims=True)
    return yc * lax.rsqrt(var + LN_EPS) * g + b


def _resident(shape):
    nd = len(shape)
    return pl.BlockSpec(shape, lambda *_: (0,) * nd, pipeline_mode=pl.Buffered(1))


def _ada_kernel(c_ref, w_ref, b_ref, o_ref):
    cs = _silu(c_ref[...]).astype(BF16)
    o_ref[...] = _dot(cs, w_ref[...].astype(BF16)) + b_ref[...]


def _ada_call(c, w_ada, b_ada):
    bsz, d = c.shape
    n = w_ada.shape[1]
    bn = n // 8
    return pl.pallas_call(
        _ada_kernel,
        grid=(n // bn,),
        in_specs=[pl.BlockSpec((bsz, d), lambda j: (0, 0)),
                  pl.BlockSpec((d, bn), lambda j: (0, j)),
                  pl.BlockSpec((1, bn), lambda j: (0, j))],
        out_specs=pl.BlockSpec((bsz, bn), lambda j: (0, j)),
        out_shape=jax.ShapeDtypeStruct((bsz, n), F32),
        compiler_params=pltpu.CompilerParams(dimension_semantics=("arbitrary",)),
        name="ada_mod",
    )(c, w_ada, b_ada.reshape(1, n))


def _s5_prep_kernel(ar_ref, ai_ref, ldt_ref, br_ref, bi_ref,
                    abr_ref, abi_ref, bbr_ref, bbi_ref):
    ar, ai = ar_ref[...], ai_ref[...]
    dt = jnp.exp(ldt_ref[...])
    mag = jnp.exp(dt * ar)
    ab_re = mag * jnp.cos(dt * ai)
    ab_im = mag * jnp.sin(dt * ai)
    den = ar * ar + ai * ai
    nr, ni = ab_re - 1.0, ab_im
    f_re = (nr * ar + ni * ai) / den
    f_im = (ni * ar - nr * ai) / den
    br, bi = br_ref[...], bi_ref[...]
    abr_ref[...] = ab_re
    abi_ref[...] = ab_im
    bbr_ref[...] = f_re * br - f_im * bi
    bbi_ref[...] = f_re * bi + f_im * br


def _s5_prep_call(a_re, a_im, log_dt, b_re, b_im):
    g, p = a_re.shape
    h = b_re.shape[-1]
    n = g * p
    row = lambda v: v.reshape(1, n)
    to_hn = lambda v: v.reshape(n, h).T
    vec = jax.ShapeDtypeStruct((1, n), F32)
    mat = jax.ShapeDtypeStruct((h, n), F32)
    return pl.pallas_call(
        _s5_prep_kernel,
        out_shape=(vec, vec, mat, mat),
        name="s5_prep",
    )(row(a_re), row(a_im), row(jnp.repeat(log_dt, p)), to_hn(b_re), to_hn(b_im))


def _swiglu_ln(x2, h2, gate2, w1_ref, w3_ref, w2_ref, g_ref, b_ref, alpha):
    hb = h2.astype(BF16)
    acc = None
    for j in range(w1_ref.shape[0]):
        a = _dot(hb, w1_ref[j])
        b = _dot(hb, w3_ref[j])
        gg = (_silu(a) * b).astype(BF16)
        part = _dot(gg, w2_ref[j])
        acc = part if acc is None else acc + part
    y = alpha * x2 + (0.5 * gate2) * acc
    return _layer_norm(y, g_ref[...], b_ref[...])


def _ffn1_kernel(x_ref, mod_ref, w1_ref, w3_ref, w2_ref, g_ref, b_ref, o_ref, tmp_ref, *, alpha):
    nb, tt, d = x_ref.shape
    sh, sc, gt = mod_ref[0], mod_ref[1], mod_ref[2]
    x3 = x_ref[...]
    h3 = x3 * (1.0 + sc)[:, None, :] + sh[:, None, :]
    gate3 = jnp.broadcast_to(gt[:, None, :], (nb, tt, d))
    out = _swiglu_ln(x3.reshape(nb * tt, d), h3.reshape(nb * tt, d),
                     gate3.reshape(nb * tt, d), w1_ref, w3_ref, w2_ref, g_ref, b_ref, alpha)
    for j in range(d // LANES):
        for b in range(nb):
            tmp_ref[j, pl.ds(b, tt, stride=nb), :] = out[b * tt:(b + 1) * tt, j * LANES:(j + 1) * LANES]
        o_ref[:, j * LANES:(j + 1) * LANES] = tmp_ref[j]


def _ffn2_kernel(x_ref, y_ref, mod2_ref, mod3_ref, wo_ref, g2_ref, b2_ref,
                 w1_ref, w3_ref, w2_ref, g3_ref, b3_ref, o_ref, tmp_ref, *, alpha):
    nb, tt, d = o_ref.shape
    x1 = x_ref[...].reshape(tt, nb, d)
    m = _dot(y_ref[...], wo_ref[...]).reshape(tt, nb, d)
    x2 = _layer_norm((alpha * x1 + mod2_ref[2][None] * m).reshape(tt * nb, d),
                     g2_ref[...], b2_ref[...])
    sh, sc, gt = mod3_ref[0], mod3_ref[1], mod3_ref[2]
    h3 = x2.reshape(tt, nb, d) * (1.0 + sc)[None] + sh[None]
    gate3 = jnp.broadcast_to(gt[None], (tt, nb, d))
    out = _swiglu_ln(x2, h3.reshape(tt * nb, d), gate3.reshape(tt * nb, d),
                     w1_ref, w3_ref, w2_ref, g3_ref, b3_ref, alpha)
    for j in range(d // LANES):
        tmp_ref[j] = out[:, j * LANES:(j + 1) * LANES]
        for b in range(nb):
            o_ref[b, :, j * LANES:(j + 1) * LANES] = tmp_ref[j, pl.ds(b, tt, stride=nb), :]


def _ffn_chunks(d_ff):
    for n in (2, 4, 11, 22):
        if d_ff % (n * LANES) == 0:
            return n
    return 1


def _ffn_weights(w1, w3, w2):
    d, f = w1.shape
    n = _ffn_chunks(f)
    ck = f // n
    w1c = w1.astype(BF16).reshape(d, n, ck).transpose(1, 0, 2)
    w3c = w3.astype(BF16).reshape(d, n, ck).transpose(1, 0, 2)
    w2c = w2.astype(BF16).reshape(n, ck, d)
    return w1c, w3c, w2c


def _ffn1_call(x, mod1, w1, w3, w2, ln_g, ln_b, alpha):
    bsz, s, d = x.shape
    nh = bsz // NB
    tt = FFN_TT
    w1c, w3c, w2c = _ffn_weights(w1, w3, w2)
    row = lambda v: v.reshape(1, d)
    return pl.pallas_call(
        functools.partial(_ffn1_kernel, alpha=alpha),
        grid=(nh, s // tt),
        in_specs=[pl.BlockSpec((NB, tt, d), lambda h, i: (h, i, 0)),
                  pl.BlockSpec((None, 3, NB, d), lambda h, i: (h, 0, 0, 0)),
                  _resident(w1c.shape), _resident(w3c.shape), _resident(w2c.shape),
                  _resident((1, d)), _resident((1, d))],
        out_specs=pl.BlockSpec((None, tt * NB, d), lambda h, i: (h, i, 0)),
        out_shape=jax.ShapeDtypeStruct((nh, s * NB, d), F32),
        scratch_shapes=[pltpu.VMEM((d // LANES, tt * NB, LANES), F32)],
        compiler_params=pltpu.CompilerParams(
            dimension_semantics=("arbitrary", "arbitrary"), vmem_limit_bytes=VMEM_LIMIT_BYTES),
        name="ffn1_ln1",
    )(x, mod1, w1c, w3c, w2c, row(ln_g), row(ln_b))


def _ffn2_call(x1t, ycat, mod2, mod3, w_out, ln2_g, ln2_b, w1, w3, w2, ln3_g, ln3_b, alpha):
    nh, rows, d = x1t.shape
    s = rows // NB
    tt = FFN_TT
    w1c, w3c, w2c = _ffn_weights(w1, w3, w2)
    row = lambda v: v.reshape(1, d)
    return pl.pallas_call(
        functools.partial(_ffn2_kernel, alpha=alpha),
        grid=(nh, s // tt),
        in_specs=[pl.BlockSpec((None, tt * NB, d), lambda h, i: (h, i, 0)),
                  pl.BlockSpec((None, tt * NB, d), lambda h, i: (h, i, 0)),
                  pl.BlockSpec((None, 3, NB, d), lambda h, i: (h, 0, 0, 0)),
                  pl.BlockSpec((None, 3, NB, d), lambda h, i: (h, 0, 0, 0)),
                  _resident(w_out.shape), _resident((1, d)), _resident((1, d)),
                  _resident(w1c.shape), _resident(w3c.shape), _resident(w2c.shape),
                  _resident((1, d)), _resident((1, d))],
        out_specs=pl.BlockSpec((NB, tt, d), lambda h, i: (h, i, 0)),
        out_shape=jax.ShapeDtypeStruct((nh * NB, s, d), F32),
        scratch_shapes=[pltpu.VMEM((d // LANES, tt * NB, LANES), F32)],
        compiler_params=pltpu.CompilerParams(
            dimension_semantics=("arbitrary", "arbitrary"), vmem_limit_bytes=VMEM_LIMIT_BYTES),
        name="outproj_ln2_ffn2_ln3",
    )(x1t, ycat, mod2, mod3, w_out.astype(BF16), row(ln2_g), row(ln2_b),
      w1c, w3c, w2c, row(ln3_g), row(ln3_b))


def _mixer_kernel(x_ref, mod_ref, wz_ref, wx_ref, wu_ref, wdt_ref, cw_ref, cb_ref,
                  dtb_ref, alog_ref, dfull_ref, nw_ref, hexp_ref,
                  abr_ref, abi_ref, bmat_ref, cmat_ref, s5d_ref, wg_ref, bg_ref,
                  o_ref,
                  z_buf, xbc_buf, act_buf, u_buf, dt_buf, y_buf, hstate, bu_buf, s5_state,
                  *, n_heads, n_groups):
    rows, d = x_ref.shape
    nb = NB
    tc = rows // nb
    halo = (SSD_CONV - 1) * nb
    ssd_w = z_buf.shape[1]
    conv_ch = xbc_buf.shape[1]
    xp = ssd_w // LANES
    assert SSD_STATE == LANES
    hpg = n_heads // n_groups
    gw = hpg * SSD_HEAD_DIM
    s5_w = u_buf.shape[1]
    c_idx = pl.program_id(1)

    @pl.when(c_idx == 0)
    def _():
        xbc_buf[0:halo, :] = jnp.zeros((halo, conv_ch), F32)
        hstate[...] = jnp.zeros(hstate.shape, F32)
        s5_state[...] = jnp.zeros(s5_state.shape, F32)

    sh, sc = mod_ref[0], mod_ref[1]
    hb = (x_ref[...].reshape(tc, nb, d) * (1.0 + sc)[None] + sh[None]).reshape(rows, d).astype(BF16)
    z_buf[...] = _dot(hb, wz_ref[...])
    xbc_buf[halo:halo + rows, :] = _dot(hb, wx_ref[...])
    u_buf[...] = _dot(hb, wu_ref[...])
    dt_raw = _dot(hb, wdt_ref[...])
    dt_buf[...] = _softplus(dt_raw + dtb_ref[...])

    for j in range(conv_ch // LANES):
        cs = slice(j * LANES, (j + 1) * LANES)
        conv = cb_ref[:, cs]
        for k in range(SSD_CONV):
            conv = conv + cw_ref[k:k + 1, cs] * xbc_buf[k * nb:k * nb + rows, cs]
        act_buf[j] = _silu(conv)
    xbc_buf[0:halo, :] = xbc_buf[rows:rows + halo, :]

    a_row = -jnp.exp(alog_ref[...])
    li = lax.broadcasted_iota(jnp.int32, (tc, tc), 0)
    si = lax.broadcasted_iota(jnp.int32, (tc, tc), 1)
    causal = li >= si
    tril = causal.astype(F32)
    hexp = hexp_ref[...]

    def ssd_one(b, carry):
        rsel = pl.ds(b, tc, stride=nb)
        xs = jnp.concatenate([act_buf[j, rsel, :] for j in range(xp)], axis=-1)
        dtb = dt_buf[rsel, :]
        acs = _dot_exact(tril, dtb * a_row)
        acs_t = acs.T
        acs_full = _dot_exact(acs, hexp)
        dt_full = _dot_exact(dtb, hexp)
        last_full = acs_full[tc - 1:tc, :]
        xin = xs * dt_full
        xdec = (xin * jnp.exp(last_full - acs_full)).astype(BF16)
        e_acs = jnp.exp(acs_full)
        cdec = jnp.exp(last_full)
        xin_b = xin.astype(BF16)
        for g in range(n_groups):
            bm = act_buf[xp + g, rsel, :]
            cm = act_buf[xp + n_groups + g, rsel, :]
            bm_b, cm_b = bm.astype(BF16), cm.astype(BF16)
            cbm = lax.dot_general(cm_b, bm_b, (((1,), (1,)), ((), ())),
                                  preferred_element_type=F32)
            h_prev = hstate[b, g]
            y_off = _dot(cm_b, h_prev.astype(BF16)) * e_acs[:, g * gw:(g + 1) * gw]
            y_parts = []
            for z in range(hpg):
                hd = g * hpg + z
                seg = acs[:, hd:hd + 1] - acs_t[hd:hd + 1, :]
                lm = jnp.where(causal, jnp.exp(seg), 0.0)
                mm = (cbm * lm).astype(BF16)
                y_parts.append(_dot(mm, xin_b[:, hd * SSD_HEAD_DIM:(hd + 1) * SSD_HEAD_DIM]))
            y_g = jnp.concatenate(y_parts, axis=-1) + y_off
            for jj in range(gw // LANES):
                y_buf[g * (gw // LANES) + jj, rsel, :] = y_g[:, jj * LANES:(jj + 1) * LANES]
            st = _dot(bm.T.astype(BF16), xdec[:, g * gw:(g + 1) * gw])
            hstate[b, g] = cdec[:, g * gw:(g + 1) * gw] * h_prev + st
        return carry

    lax.fori_loop(0, nb, ssd_one, 0)

    y = jnp.concatenate([y_buf[j] + dfull_ref[:, j * LANES:(j + 1) * LANES] * act_buf[j]
                         for j in range(xp)], axis=-1)
    y = y * _silu(z_buf[...])
    nw = ssd_w // n_groups
    parts = []
    for g in range(n_groups):
        yg = y[:, g * nw:(g + 1) * nw]
        parts.append(yg * lax.rsqrt(jnp.mean(yg * yg, axis=-1, keepdims=True) + LN_EPS))
    o_ref[:, 0:ssd_w] = (jnp.concatenate(parts, axis=-1) * nw_ref[...]).astype(o_ref.dtype)

    n_half = bmat_ref.shape[0]
    hk = bmat_ref.shape[1]
    sk = bmat_ref.shape[2] // 2
    sub_rows = S5_SUB_T * nb
    for sb in range(tc // S5_SUB_T):
        r0 = sb * sub_rows
        u_sb = u_buf[r0:r0 + sub_rows, :]
        ub = u_sb.astype(BF16)
        for k in range(n_half):
            bu_buf[:, k * 2 * sk:(k + 1) * 2 * sk] = _dot(ub[:, k * hk:(k + 1) * hk], bmat_ref[k])
        for k in range(n_half):
            a_r, a_i = abr_ref[k], abi_ref[k]
            c_re, c_im = k * 2 * sk, k * 2 * sk + sk

            def step(t, xc, a_r=a_r, a_i=a_i, c_re=c_re, c_im=c_im):
                xr, xi = xc
                rr = pl.ds(pl.multiple_of(t * nb, nb), nb)
                nxr = a_r * xr - a_i * xi + bu_buf[rr, c_re:c_re + sk]
                nxi = a_r * xi + a_i * xr + bu_buf[rr, c_im:c_im + sk]
                bu_buf[rr, c_re:c_re + sk] = nxr
                bu_buf[rr, c_im:c_im + sk] = nxi
                return nxr, nxi

            xr, xi = lax.fori_loop(0, S5_SUB_T, step, (s5_state[k, 0], s5_state[k, 1]))
            s5_state[k, 0] = xr
            s5_state[k, 1] = xi
        ys = [_dot(bu_buf[:, k * 2 * sk:(k + 1) * 2 * sk].astype(BF16), cmat_ref[k])
              for k in range(n_half)]
        y5 = jnp.concatenate(ys, axis=-1) + u_sb * s5d_ref[...]
        gl = jax.nn.gelu(y5, approximate=True)
        gate = _sigmoid(_dot(gl.astype(BF16), wg_ref[...]) + bg_ref[...])
        o_ref[r0:r0 + sub_rows, ssd_w:ssd_w + s5_w] = (gl * gate).astype(o_ref.dtype)


def _mixer_call(x1t, mod2, w_in, conv_w, conv_b, dt_bias, a_log, d_ssd, ssd_norm_w,
                ab_re, ab_im, bmat, cmat, s5_d, w_glu, b_glu):
    nh, rows_total, d = x1t.shape
    s = rows_total // NB
    ssd_w = ssd_norm_w.shape[0]
    n_heads = d_ssd.shape[0]
    conv_ch = conv_w.shape[1]
    s5_w = s5_d.shape[0]
    n_groups = SSD_GROUPS
    tc = SSD_CHUNK
    rows = tc * NB
    halo = (SSD_CONV - 1) * NB

    o1, o2, o3 = ssd_w, ssd_w + conv_ch, ssd_w + conv_ch + n_heads
    w_z = w_in[:, :o1].astype(BF16)
    w_x = w_in[:, o1:o2].astype(BF16)
    w_dt = jnp.pad(w_in[:, o2:o3], ((0, 0), (0, LANES - n_heads))).astype(BF16)
    w_u = w_in[:, o3:].astype(BF16)
    pad_h = lambda v: jnp.pad(v, (0, LANES - n_heads)).reshape(1, LANES)
    hexp = (jnp.arange(LANES)[:, None] == (jnp.arange(ssd_w) // SSD_HEAD_DIM)[None, :]).astype(F32)
    d_full = jnp.repeat(d_ssd, SSD_HEAD_DIM).reshape(1, ssd_w)

    n_half, hk, sk2 = bmat.shape
    kern = functools.partial(_mixer_kernel, n_heads=n_heads, n_groups=n_groups)
    xmap = lambda h, c: (h, c, 0)
    return pl.pallas_call(
        kern,
        grid=(nh, s // tc),
        in_specs=[pl.BlockSpec((None, rows, d), xmap),
                  pl.BlockSpec((None, 3, NB, d), lambda h, c: (h, 0, 0, 0)),
                  _resident(w_z.shape), _resident(w_x.shape), _resident(w_u.shape),
                  _resident(w_dt.shape),
                  _resident(conv_w.shape), _resident((1, conv_ch)),
                  _resident((1, LANES)), _resident((1, LANES)),
                  _resident((1, ssd_w)), _resident((1, ssd_w)), _resident(hexp.shape),
                  _resident(ab_re.shape), _resident(ab_im.shape),
                  _resident(bmat.shape), _resident(cmat.shape),
                  _resident((1, s5_w)), _resident(w_glu.shape), _resident((1, s5_w))],
        out_specs=pl.BlockSpec((None, rows, ssd_w + s5_w), xmap),
        out_shape=jax.ShapeDtypeStruct((nh, rows_total, ssd_w + s5_w), BF16),
        scratch_shapes=[
            pltpu.VMEM((rows, ssd_w), F32),
            pltpu.VMEM((rows + halo, conv_ch), F32),
            pltpu.VMEM((conv_ch // LANES, rows, LANES), F32),
            pltpu.VMEM((rows, s5_w), F32),
            pltpu.VMEM((rows, LANES), F32),
            pltpu.VMEM((ssd_w // LANES, rows, LANES), F32),
            pltpu.VMEM((NB, n_groups, SSD_STATE, ssd_w // n_groups), F32),
            pltpu.VMEM((S5_SUB_T * NB, n_half * sk2), F32),
            pltpu.VMEM((n_half, 2, NB, sk2 // 2), F32),
        ],
        compiler_params=pltpu.CompilerParams(
            dimension_semantics=("arbitrary", "arbitrary"), vmem_limit_bytes=VMEM_LIMIT_BYTES),
        name="mixer",
    )(x1t, mod2, w_z, w_x, w_u, w_dt, conv_w, conv_b.reshape(1, conv_ch),
      pad_h(dt_bias), pad_h(a_log), d_full, ssd_norm_w.reshape(1, ssd_w), hexp,
      ab_re, ab_im, bmat, cmat, s5_d.reshape(1, s5_w), w_glu.astype(BF16),
      b_glu.reshape(1, s5_w))


def _s5_matrices(ab_re, ab_im, bb_re, bb_im, c_re, c_im):
    h, n = bb_re.shape
    g, _, p = c_re.shape
    n_half = 2
    gh = g // n_half
    eye = jnp.eye(gh, dtype=F32)

    def b_block(bb):
        v = bb.reshape(h, n_half, gh, p)
        return jnp.einsum("hkgp,gG->kghGp", v, eye).reshape(n_half, gh * h, gh * p)

    def c_block(c):
        v = c.reshape(n_half, gh, h, p)
        return jnp.einsum("kghp,gG->kgpGh", v, eye).reshape(n_half, gh * p, gh * h)

    bmat = jnp.concatenate([b_block(bb_re), b_block(bb_im)], axis=-1).astype(BF16)
    cmat = jnp.concatenate([c_block(c_re), c_block(-c_im)], axis=1).astype(BF16)
    bc = lambda v: jnp.broadcast_to(v.reshape(n_half, 1, n // n_half), (n_half, NB, n // n_half))
    return bc(ab_re), bc(ab_im), bmat, cmat


def kernel(x, c, w_ada, b_ada, ffn1_w1, ffn1_w3, ffn1_w2, ln1_g, ln1_b, w_in, conv_w, conv_b,
           dt_bias, a_log, d_ssd, ssd_norm_w, s5_a_re, s5_a_im, s5_log_dt, s5_b_re, s5_b_im,
           s5_c_re, s5_c_im, s5_d, w_glu, b_glu, w_out, ln2_g, ln2_b, ffn2_w1, ffn2_w3, ffn2_w2,
           ln3_g, ln3_b):
    bsz, s, d = x.shape
    depth = w_ada.shape[0]
    alpha = (2 * depth) ** 0.25
    assert depth == 1
    assert bsz % NB == 0 and s % SSD_CHUNK == 0 and s % FFN_TT == 0
    nh = bsz // NB
    for l in range(depth):
        mod = _ada_call(c, w_ada[l], b_ada[l]).reshape(bsz, N_MOD // 3, 3, d)
        mod = mod.reshape(nh, NB, N_MOD // 3, 3, d).transpose(2, 0, 3, 1, 4)
        ab_re, ab_im, bb_re, bb_im = _s5_prep_call(s5_a_re[l], s5_a_im[l], s5_log_dt[l],
                                                   s5_b_re[l], s5_b_im[l])
        ab_re, ab_im, bmat, cmat = _s5_matrices(ab_re, ab_im, bb_re, bb_im, s5_c_re[l], s5_c_im[l])
        x1t = _ffn1_call(x, mod[0], ffn1_w1[l], ffn1_w3[l], ffn1_w2[l], ln1_g[l], ln1_b[l], alpha)
        ycat = _mixer_call(x1t, mod[1], w_in[l], conv_w[l], conv_b[l], dt_bias[l], a_log[l],
                           d_ssd[l], ssd_norm_w[l], ab_re, ab_im, bmat, cmat, s5_d[l],
                           w_glu[l], b_glu[l])
        x = _ffn2_call(x1t, ycat, mod[1], mod[2], w_out[l], ln2_g[l], ln2_b[l],
                       ffn2_w1[l], ffn2_w3[l], ffn2_w2[l], ln3_g[l], ln3_b[l], alpha)
    return x
```

```python
import functools
import math

import jax
import jax.numpy as jnp
from jax import lax
from jax.experimental import pallas as pl
from jax.experimental.pallas import tpu as pltpu

F32 = jnp.float32
BF16 = jnp.bfloat16

LN_EPS = 1e-5
SUBLANES = 8
LANES = 128
MXU_TILE = 256
VMEM_LIMIT_BYTES = 56 * 1024 * 1024

SSD_HEAD_DIM = 64
SSD_GROUPS = 2
SSD_STATE = 128
SSD_CONV = 4
SSD_CHUNK = 128
S5_GROUP_CH = 16
S5_STATE = 64
N_MOD = 9

NB = SUBLANES
FFN_TT = 64
S5_SUB_T = 32


def _dot(a, b):
    return jnp.dot(a, b, preferred_element_type=F32)


def _sigmoid(x):
    return 1.0 / (1.0 + jnp.exp(-x))


def _silu(x):
    return x * _sigmoid(x)


def _softplus(x):
    return jnp.maximum(x, 0.0) + jnp.log1p(jnp.exp(-jnp.abs(x)))


def _layer_norm(y, g, b):
    mu = jnp.mean(y, axis=-1, keepdims=True)
    yc = y - mu
    var = jnp.mean(yc * yc, axis=-1, keepdims=True)
    return yc * lax.rsqrt(var + LN_EPS) * g + b


def _resident(shape):
    nd = len(shape)
    return pl.BlockSpec(shape, lambda *_: (0,) * nd, pipeline_mode=pl.Buffered(1))


def _ada_kernel(c_ref, w_ref, b_ref, o_ref):
    cs = _silu(c_ref[...]).astype(BF16)
    o_ref[...] = _dot(cs, w_ref[...].astype(BF16)) + b_ref[...]


def _ada_call(c, w_ada, b_ada):
    bsz, d = c.shape
    n = w_ada.shape[1]
    bn = n // 8
    return pl.pallas_call(
        _ada_kernel,
        grid=(n // bn,),
        in_specs=[pl.BlockSpec((bsz, d), lambda j: (0, 0)),
                  pl.BlockSpec((d, bn), lambda j: (0, j)),
                  pl.BlockSpec((1, bn), lambda j: (0, j))],
        out_specs=pl.BlockSpec((bsz, bn), lambda j: (0, j)),
        out_shape=jax.ShapeDtypeStruct((bsz, n), F32),
        compiler_params=pltpu.CompilerParams(dimension_semantics=("arbitrary",)),
        name="ada_mod",
    )(c, w_ada, b_ada.reshape(1, n))


def _s5_prep_kernel(ar_ref, ai_ref, ldt_ref, br_ref, bi_ref,
                    abr_ref, abi_ref, bbr_ref, bbi_ref):
    ar, ai = ar_ref[...], ai_ref[...]
    dt = jnp.exp(ldt_ref[...])
    mag = jnp.exp(dt * ar)
    ab_re = mag * jnp.cos(dt * ai)
    ab_im = mag * jnp.sin(dt * ai)
    den = ar * ar + ai * ai
    nr, ni = ab_re - 1.0, ab_im
    f_re = (nr * ar + ni * ai) / den
    f_im = (ni * ar - nr * ai) / den
    br, bi = br_ref[...], bi_ref[...]
    abr_ref[...] = ab_re
    abi_ref[...] = ab_im
    bbr_ref[...] = f_re * br - f_im * bi
    bbi_ref[...] = f_re * bi + f_im * br


def _s5_prep_call(a_re, a_im, log_dt, b_re, b_im):
    g, p = a_re.shape
    h = b_re.shape[-1]
    n = g * p
    row = lambda v: v.reshape(1, n)
    to_hn = lambda v: v.reshape(n, h).T
    vec = jax.ShapeDtypeStruct((1, n), F32)
    mat = jax.ShapeDtypeStruct((h, n), F32)
    return pl.pallas_call(
        _s5_prep_kernel,
        out_shape=(vec, vec, mat, mat),
        name="s5_prep",
    )(row(a_re), row(a_im), row(jnp.repeat(log_dt, p)), to_hn(b_re), to_hn(b_im))


def _swiglu_ln(x2, h2, gate2, w1_ref, w3_ref, w2_ref, g_ref, b_ref, alpha):
    hb = h2.astype(BF16)
    acc = None
    for lo, hi in _ffn_chunks(w1_ref.shape[1]):
        a = _dot(hb, w1_ref[:, lo:hi])
        b = _dot(hb, w3_ref[:, lo:hi])
        gg = (_silu(a) * b).astype(BF16)
        part = _dot(gg, w2_ref[lo:hi, :])
        acc = part if acc is None else acc + part
    y = alpha * x2 + (0.5 * gate2) * acc
    return _layer_norm(y, g_ref[...], b_ref[...])


def _ffn1_kernel(x_ref, mod_ref, w1_ref, w3_ref, w2_ref, g_ref, b_ref, o_ref, tmp_ref, *, alpha):
    nb, tt, d = x_ref.shape
    sh, sc, gt = mod_ref[0], mod_ref[1], mod_ref[2]
    x3 = x_ref[...]
    h3 = x3 * (1.0 + sc)[:, None, :] + sh[:, None, :]
    gate3 = jnp.broadcast_to(gt[:, None, :], (nb, tt, d))
    out = _swiglu_ln(x3.reshape(nb * tt, d), h3.reshape(nb * tt, d),
                     gate3.reshape(nb * tt, d), w1_ref, w3_ref, w2_ref, g_ref, b_ref, alpha)
    for j in range(d // LANES):
        for b in range(nb):
            tmp_ref[j, pl.ds(b, tt, stride=nb), :] = out[b * tt:(b + 1) * tt, j * LANES:(j + 1) * LANES]
        o_ref[:, j * LANES:(j + 1) * LANES] = tmp_ref[j]


def _ffn2_kernel(x_ref, y_ref, mod2_ref, mod3_ref, wo_ref, g2_ref, b2_ref,
                 w1_ref, w3_ref, w2_ref, g3_ref, b3_ref, o_ref, tmp_ref, *, alpha):
    nb, tt, d = o_ref.shape
    x1 = x_ref[...].reshape(tt, nb, d)
    m = _dot(y_ref[...], wo_ref[...]).reshape(tt, nb, d)
    x2 = _layer_norm((alpha * x1 + mod2_ref[2][None] * m).reshape(tt * nb, d),
                     g2_ref[...], b2_ref[...])
    sh, sc, gt = mod3_ref[0], mod3_ref[1], mod3_ref[2]
    h3 = x2.reshape(tt, nb, d) * (1.0 + sc)[None] + sh[None]
    gate3 = jnp.broadcast_to(gt[None], (tt, nb, d))
    out = _swiglu_ln(x2, h3.reshape(tt * nb, d), gate3.reshape(tt * nb, d),
                     w1_ref, w3_ref, w2_ref, g3_ref, b3_ref, alpha)
    for j in range(d // LANES):
        tmp_ref[j] = out[:, j * LANES:(j + 1) * LANES]
        for b in range(nb):
            o_ref[b, :, j * LANES:(j + 1) * LANES] = tmp_ref[j, pl.ds(b, tt, stride=nb), :]


def _ffn_chunks(d_ff):
    assert d_ff % MXU_TILE == 0
    cut = (d_ff // MXU_TILE + 1) // 2 * MXU_TILE
    return ((0, cut), (cut, d_ff)) if cut < d_ff else ((0, d_ff),)


def _ffn_weights(w1, w3, w2):
    return w1.astype(BF16), w3.astype(BF16), w2.astype(BF16)


def _ffn1_call(x, mod1, w1, w3, w2, ln_g, ln_b, alpha):
    bsz, s, d = x.shape
    nh = bsz // NB
    tt = FFN_TT
    w1c, w3c, w2c = _ffn_weights(w1, w3, w2)
    row = lambda v: v.reshape(1, d)
    return pl.pallas_call(
        functools.partial(_ffn1_kernel, alpha=alpha),
        grid=(nh, s // tt),
        in_specs=[pl.BlockSpec((NB, tt, d), lambda h, i: (h, i, 0)),
                  pl.BlockSpec((None, 3, NB, d), lambda h, i: (h, 0, 0, 0)),
                  _resident(w1c.shape), _resident(w3c.shape), _resident(w2c.shape),
                  _resident((1, d)), _resident((1, d))],
        out_specs=pl.BlockSpec((None, tt * NB, d), lambda h, i: (h, i, 0)),
        out_shape=jax.ShapeDtypeStruct((nh, s * NB, d), F32),
        scratch_shapes=[pltpu.VMEM((d // LANES, tt * NB, LANES), F32)],
        compiler_params=pltpu.CompilerParams(
            dimension_semantics=("arbitrary", "arbitrary"), vmem_limit_bytes=VMEM_LIMIT_BYTES),
        name="ffn1_ln1",
    )(x, mod1, w1c, w3c, w2c, row(ln_g), row(ln_b))


def _ffn2_call(x1t, ycat, mod2, mod3, w_out, ln2_g, ln2_b, w1, w3, w2, ln3_g, ln3_b, alpha):
    nh, rows, d = x1t.shape
    s = rows // NB
    tt = FFN_TT
    w1c, w3c, w2c = _ffn_weights(w1, w3, w2)
    row = lambda v: v.reshape(1, d)
    return pl.pallas_call(
        functools.partial(_ffn2_kernel, alpha=alpha),
        grid=(nh, s // tt),
        in_specs=[pl.BlockSpec((None, tt * NB, d), lambda h, i: (h, i, 0)),
                  pl.BlockSpec((None, tt * NB, d), lambda h, i: (h, i, 0)),
                  pl.BlockSpec((None, 3, NB, d), lambda h, i: (h, 0, 0, 0)),
                  pl.BlockSpec((None, 3, NB, d), lambda h, i: (h, 0, 0, 0)),
                  _resident(w_out.shape), _resident((1, d)), _resident((1, d)),
                  _resident(w1c.shape), _resident(w3c.shape), _resident(w2c.shape),
                  _resident((1, d)), _resident((1, d))],
        out_specs=pl.BlockSpec((NB, tt, d), lambda h, i: (h, i, 0)),
        out_shape=jax.ShapeDtypeStruct((nh * NB, s, d), F32),
        scratch_shapes=[pltpu.VMEM((d // LANES, tt * NB, LANES), F32)],
        compiler_params=pltpu.CompilerParams(
            dimension_semantics=("arbitrary", "arbitrary"), vmem_limit_bytes=VMEM_LIMIT_BYTES),
        name="outproj_ln2_ffn2_ln3",
    )(x1t, ycat, mod2, mod3, w_out.astype(BF16), row(ln2_g), row(ln2_b),
      w1c, w3c, w2c, row(ln3_g), row(ln3_b))


def _mixer_kernel(x_ref, mod_ref, wz_ref, wx_ref, wu_ref, wdt_ref, cw_ref, cb_ref,
                  dtb_ref, alog_ref, dfull_ref, nw_ref, hexp3_ref,
                  abr_ref, abi_ref, bmat_ref, cmat_ref, s5d_ref, wg_ref, bg_ref,
                  o_ref,
                  z_buf, xbc_buf, act_buf, u_buf, acs_buf, y_buf, xdec_buf, eacs_buf, cdec_buf,
                  hstate, bu_buf, s5_state,
                  *, n_heads, n_groups):
    rows, d = x_ref.shape
    nb = NB
    tc = rows // nb
    halo = (SSD_CONV - 1) * nb
    ssd_w = z_buf.shape[1]
    conv_ch = xbc_buf.shape[1]
    xp = ssd_w // LANES
    assert SSD_STATE == LANES
    hpg = n_heads // n_groups
    gw = hpg * SSD_HEAD_DIM
    s5_w = u_buf.shape[1]
    c_idx = pl.program_id(1)

    @pl.when(c_idx == 0)
    def _():
        xbc_buf[0:halo, :] = jnp.zeros((halo, conv_ch), F32)
        hstate[...] = jnp.zeros(hstate.shape, F32)
        s5_state[...] = jnp.zeros(s5_state.shape, F32)

    sh, sc = mod_ref[0], mod_ref[1]
    hb = (x_ref[...].reshape(tc, nb, d) * (1.0 + sc)[None] + sh[None]).reshape(rows, d).astype(BF16)
    z_buf[...] = _dot(hb, wz_ref[...])
    xbc_buf[halo:halo + rows, :] = _dot(hb, wx_ref[...])
    u_buf[...] = _dot(hb, wu_ref[...])
    dt_raw = _dot(hb, wdt_ref[...])

    for j in range(conv_ch // LANES):
        cs = slice(j * LANES, (j + 1) * LANES)
        conv = cb_ref[:, cs]
        for k in range(SSD_CONV):
            conv = conv + cw_ref[k:k + 1, cs] * xbc_buf[k * nb:k * nb + rows, cs]
        act_buf[j] = _silu(conv)
    xbc_buf[0:halo, :] = xbc_buf[rows:rows + halo, :]

    dt = _softplus(dt_raw + dtb_ref[...])
    acs = dt * (-jnp.exp(alog_ref[...]))
    shift = nb
    while shift < rows:
        acs = acs + jnp.concatenate([jnp.zeros((shift, LANES), F32), acs[:rows - shift]], axis=0)
        shift *= 2
    acs_buf[...] = acs
    hi = acs.astype(BF16)
    r1 = acs - hi.astype(F32)
    mid = r1.astype(BF16)
    lo = (r1 - mid.astype(F32)).astype(BF16)
    acs_full = _dot(jnp.concatenate([hi, mid, lo], axis=-1), hexp3_ref[...])
    dt_full = _dot(dt.astype(BF16), hexp3_ref[0:LANES, :])
    last = acs_full[rows - nb:rows, :]
    cdec_buf[...] = jnp.exp(last)
    dec = jnp.exp(last[None] - acs_full.reshape(tc, nb, ssd_w)).reshape(rows, ssd_w)
    eac = jnp.exp(acs_full)
    for j in range(xp):
        cs = slice(j * LANES, (j + 1) * LANES)
        xs = act_buf[j]
        xin = xs * dt_full[:, cs]
        y_buf[j] = dfull_ref[:, cs] * xs
        act_buf[j] = xin
        xdec_buf[j] = xin * dec[:, cs]
        eacs_buf[j] = eac[:, cs]

    causal = (lax.broadcasted_iota(jnp.int32, (tc, tc), 0)
              >= lax.broadcasted_iota(jnp.int32, (tc, tc), 1))
    gp = gw // LANES
    lane_head = lax.broadcasted_iota(jnp.int32, (tc, gw), 1) // SSD_HEAD_DIM

    def ssd_one(b, carry):
        rsel = pl.ds(b, tc, stride=nb)
        acs_b = acs_buf[rsel, :]
        acs_t = acs_b.T
        cdec = cdec_buf[pl.ds(b, 1), :]
        for g in range(n_groups):
            gather = lambda buf: jnp.concatenate(
                [buf[g * gp + jj, rsel, :] for jj in range(gp)], axis=-1)
            xin_g, xdec_g, eacs_g = gather(act_buf), gather(xdec_buf), gather(eacs_buf)
            bm = act_buf[xp + g, rsel, :]
            cm_b = act_buf[xp + n_groups + g, rsel, :].astype(BF16)
            cbm = lax.dot_general(cm_b, bm.astype(BF16), (((1,), (1,)), ((), ())),
                                  preferred_element_type=F32)
            h_prev = hstate[b, g]
            y_off = _dot(cm_b, h_prev.astype(BF16)) * eacs_g
            mms, xblk = [], []
            for z in range(hpg):
                hd = g * hpg + z
                seg = acs_b[:, hd:hd + 1] - acs_t[hd:hd + 1, :]
                lm = jnp.where(causal, jnp.exp(seg), 0.0)
                mms.append((cbm * lm).astype(BF16))
                xblk.append(jnp.where(lane_head == z, xin_g, 0.0).astype(BF16))
            y_g = _dot(jnp.concatenate(mms, axis=-1), jnp.concatenate(xblk, axis=0)) + y_off
            for jj in range(gp):
                y_buf[g * gp + jj, rsel, :] = (y_buf[g * gp + jj, rsel, :]
                                               + y_g[:, jj * LANES:(jj + 1) * LANES])
            st = _dot(bm.T.astype(BF16), xdec_g.astype(BF16))
            hstate[b, g] = cdec[:, g * gw:(g + 1) * gw] * h_prev + st
        return carry

    lax.fori_loop(0, nb, ssd_one, 0)

    y = jnp.concatenate([y_buf[j] for j in range(xp)], axis=-1)
    y = y * _silu(z_buf[...])
    nw = ssd_w // n_groups
    parts = []
    for g in range(n_groups):
        yg = y[:, g * nw:(g + 1) * nw]
        parts.append(yg * lax.rsqrt(jnp.mean(yg * yg, axis=-1, keepdims=True) + LN_EPS))
    o_ref[:, 0:ssd_w] = (jnp.concatenate(parts, axis=-1) * nw_ref[...]).astype(o_ref.dtype)

    n_half = bmat_ref.shape[0]
    hk = bmat_ref.shape[1]
    sk = bmat_ref.shape[2] // 2
    sub_rows = S5_SUB_T * nb
    for sb in range(tc // S5_SUB_T):
        r0 = sb * sub_rows
        u_sb = u_buf[r0:r0 + sub_rows, :]
        ub = u_sb.astype(BF16)
        for k in range(n_half):
            bu_buf[:, k * 2 * sk:(k + 1) * 2 * sk] = _dot(ub[:, k * hk:(k + 1) * hk], bmat_ref[k])
        for k in range(n_half):
            a_r, a_i = abr_ref[k], abi_ref[k]
            c_re, c_im = k * 2 * sk, k * 2 * sk + sk

            def step(t, xc, a_r=a_r, a_i=a_i, c_re=c_re, c_im=c_im):
                xr, xi = xc
                rr = pl.ds(pl.multiple_of(t * nb, nb), nb)
                nxr = a_r * xr - a_i * xi + bu_buf[rr, c_re:c_re + sk]
                nxi = a_r * xi + a_i * xr + bu_buf[rr, c_im:c_im + sk]
                bu_buf[rr, c_re:c_re + sk] = nxr
                bu_buf[rr, c_im:c_im + sk] = nxi
                return nxr, nxi

            xr, xi = lax.fori_loop(0, S5_SUB_T, step, (s5_state[k, 0], s5_state[k, 1]))
            s5_state[k, 0] = xr
            s5_state[k, 1] = xi
        ys = [_dot(bu_buf[:, k * 2 * sk:(k + 1) * 2 * sk].astype(BF16), cmat_ref[k])
              for k in range(n_half)]
        y5 = jnp.concatenate(ys, axis=-1) + u_sb * s5d_ref[...]
        gl = jax.nn.gelu(y5, approximate=True)
        gate = _sigmoid(_dot(gl.astype(BF16), wg_ref[...]) + bg_ref[...])
        o_ref[r0:r0 + sub_rows, ssd_w:ssd_w + s5_w] = (gl * gate).astype(o_ref.dtype)


def _mixer_call(x1t, mod2, w_in, conv_w, conv_b, dt_bias, a_log, d_ssd, ssd_norm_w,
                ab_re, ab_im, bmat, cmat, s5_d, w_glu, b_glu):
    nh, rows_total, d = x1t.shape
    s = rows_total // NB
    ssd_w = ssd_norm_w.shape[0]
    n_heads = d_ssd.shape[0]
    conv_ch = conv_w.shape[1]
    s5_w = s5_d.shape[0]
    n_groups = SSD_GROUPS
    tc = SSD_CHUNK
    rows = tc * NB
    halo = (SSD_CONV - 1) * NB

    o1, o2, o3 = ssd_w, ssd_w + conv_ch, ssd_w + conv_ch + n_heads
    w_z = w_in[:, :o1].astype(BF16)
    w_x = w_in[:, o1:o2].astype(BF16)
    w_dt = jnp.pad(w_in[:, o2:o3], ((0, 0), (0, LANES - n_heads))).astype(BF16)
    w_u = w_in[:, o3:].astype(BF16)
    pad_h = lambda v: jnp.pad(v, (0, LANES - n_heads)).reshape(1, LANES)
    hexp = (jnp.arange(LANES)[:, None] == (jnp.arange(ssd_w) // SSD_HEAD_DIM)[None, :]).astype(BF16)
    hexp = jnp.concatenate([hexp] * 3, axis=0)
    d_full = jnp.repeat(d_ssd, SSD_HEAD_DIM).reshape(1, ssd_w)

    n_half, hk, sk2 = bmat.shape
    kern = functools.partial(_mixer_kernel, n_heads=n_heads, n_groups=n_groups)
    xmap = lambda h, c: (h, c, 0)
    return pl.pallas_call(
        kern,
        grid=(nh, s // tc),
        in_specs=[pl.BlockSpec((None, rows, d), xmap),
                  pl.BlockSpec((None, 3, NB, d), lambda h, c: (h, 0, 0, 0)),
                  _resident(w_z.shape), _resident(w_x.shape), _resident(w_u.shape),
                  _resident(w_dt.shape),
                  _resident(conv_w.shape), _resident((1, conv_ch)),
                  _resident((1, LANES)), _resident((1, LANES)),
                  _resident((1, ssd_w)), _resident((1, ssd_w)), _resident(hexp.shape),
                  _resident(ab_re.shape), _resident(ab_im.shape),
                  _resident(bmat.shape), _resident(cmat.shape),
                  _resident((1, s5_w)), _resident(w_glu.shape), _resident((1, s5_w))],
        out_specs=pl.BlockSpec((None, rows, ssd_w + s5_w), xmap),
        out_shape=jax.ShapeDtypeStruct((nh, rows_total, ssd_w + s5_w), BF16),
        scratch_shapes=[
            pltpu.VMEM((rows, ssd_w), F32),
            pltpu.VMEM((rows + halo, conv_ch), F32),
            pltpu.VMEM((conv_ch // LANES, rows, LANES), F32),
            pltpu.VMEM((rows, s5_w), F32),
            pltpu.VMEM((rows, LANES), F32),
            pltpu.VMEM((ssd_w // LANES, rows, LANES), F32),
            pltpu.VMEM((ssd_w // LANES, rows, LANES), F32),
            pltpu.VMEM((ssd_w // LANES, rows, LANES), F32),
            pltpu.VMEM((NB, ssd_w), F32),
            pltpu.VMEM((NB, n_groups, SSD_STATE, ssd_w // n_groups), F32),
            pltpu.VMEM((S5_SUB_T * NB, n_half * sk2), F32),
            pltpu.VMEM((n_half, 2, NB, sk2 // 2), F32),
        ],
        compiler_params=pltpu.CompilerParams(
            dimension_semantics=("arbitrary", "arbitrary"), vmem_limit_bytes=VMEM_LIMIT_BYTES),
        name="mixer",
    )(x1t, mod2, w_z, w_x, w_u, w_dt, conv_w, conv_b.reshape(1, conv_ch),
      pad_h(dt_bias), pad_h(a_log), d_full, ssd_norm_w.reshape(1, ssd_w), hexp,
      ab_re, ab_im, bmat, cmat, s5_d.reshape(1, s5_w), w_glu.astype(BF16),
      b_glu.reshape(1, s5_w))


def _s5_matrices(ab_re, ab_im, bb_re, bb_im, c_re, c_im):
    h, n = bb_re.shape
    g, _, p = c_re.shape
    n_half = 2
    gh = g // n_half
    eye = jnp.eye(gh, dtype=F32)

    def b_block(bb):
        v = bb.reshape(h, n_half, gh, p)
        return jnp.einsum("hkgp,gG->kghGp", v, eye).reshape(n_half, gh * h, gh * p)

    def c_block(c):
        v = c.reshape(n_half, gh, h, p)
        return jnp.einsum("kghp,gG->kgpGh", v, eye).reshape(n_half, gh * p, gh * h)

    bmat = jnp.concatenate([b_block(bb_re), b_block(bb_im)], axis=-1).astype(BF16)
    cmat = jnp.concatenate([c_block(c_re), c_block(-c_im)], axis=1).astype(BF16)
    bc = lambda v: jnp.broadcast_to(v.reshape(n_half, 1, n // n_half), (n_half, NB, n // n_half))
    return bc(ab_re), bc(ab_im), bmat, cmat


def kernel(x, c, w_ada, b_ada, ffn1_w1, ffn1_w3, ffn1_w2, ln1_g, ln1_b, w_in, conv_w, conv_b,
           dt_bias, a_log, d_ssd, ssd_norm_w, s5_a_re, s5_a_im, s5_log_dt, s5_b_re, s5_b_im,
           s5_c_re, s5_c_im, s5_d, w_glu, b_glu, w_out, ln2_g, ln2_b, ffn2_w1, ffn2_w3, ffn2_w2,
           ln3_g, ln3_b):
    bsz, s, d = x.shape
    depth = w_ada.shape[0]
    alpha = (2 * depth) ** 0.25
    assert depth == 1
    assert bsz % NB == 0 and s % SSD_CHUNK == 0 and s % FFN_TT == 0
    nh = bsz // NB
    for l in range(depth):
        mod = _ada_call(c, w_ada[l], b_ada[l]).reshape(bsz, N_MOD // 3, 3, d)
        mod = mod.reshape(nh, NB, N_MOD // 3, 3, d).transpose(2, 0, 3, 1, 4)
        ab_re, ab_im, bb_re, bb_im = _s5_prep_call(s5_a_re[l], s5_a_im[l], s5_log_dt[l],
                                                   s5_b_re[l], s5_b_im[l])
        ab_re, ab_im, bmat, cmat = _s5_matrices(ab_re, ab_im, bb_re, bb_im, s5_c_re[l], s5_c_im[l])
        x1t = _ffn1_call(x, mod[0], ffn1_w1[l], ffn1_w3[l], ffn1_w2[l], ln1_g[l], ln1_b[l], alpha)
        ycat = _mixer_call(x1t, mod[1], w_in[l], conv_w[l], conv_b[l], dt_bias[l], a_log[l],
                           d_ssd[l], ssd_norm_w[l], ab_re, ab_im, bmat, cmat, s5_d[l],
                           w_glu[l], b_glu[l])
        x = _ffn2_call(x1t, ycat, mod[1], mod[2], w_out[l], ln2_g[l], ln2_b[l],
                       ffn2_w1[l], ffn2_w3[l], ffn2_w2[l], ln3_g[l], ln3_b[l], alpha)
    return x
```

```python
import functools
import math

import jax
import jax.numpy as jnp
from jax import lax
from jax.experimental import pallas as pl
from jax.experimental.pallas import tpu as pltpu

F32 = jnp.float32
BF16 = jnp.bfloat16

LN_EPS = 1e-5
SUBLANES = 8
LANES = 128
MXU_TILE = 256
VMEM_LIMIT_BYTES = 56 * 1024 * 1024

SSD_HEAD_DIM = 64
SSD_GROUPS = 2
SSD_STATE = 128
SSD_CONV = 4
SSD_CHUNK = 128
S5_GROUP_CH = 16
S5_STATE = 64
N_MOD = 9

NB = SUBLANES
FFN_TT = 64
S5_SUB_T = 32


def _dot(a, b):
    return jnp.dot(a, b, preferred_element_type=F32)


def _sigmoid(x):
    return 1.0 / (1.0 + jnp.exp(-x))


def _silu(x):
    return x * _sigmoid(x)


def _softplus(x):
    return jnp.maximum(x, 0.0) + jnp.log1p(jnp.exp(-jnp.abs(x)))


def _layer_norm(y, g, b):
    mu = jnp.mean(y, axis=-1, keepdims=True)
    yc = y - mu
    var = jnp.mean(yc * yc, axis=-1, keepdims=True)
    return yc * lax.rsqrt(var + LN_EPS) * g + b


def _resident(shape):
    nd = len(shape)
    return pl.BlockSpec(shape, lambda *_: (0,) * nd, pipeline_mode=pl.Buffered(1))


def _ada_kernel(c_ref, w_ref, b_ref, o_ref):
    cs = _silu(c_ref[...]).astype(BF16)
    o_ref[...] = _dot(cs, w_ref[...].astype(BF16)) + b_ref[...]


def _ada_call(c, w_ada, b_ada):
    bsz, d = c.shape
    n = w_ada.shape[1]
    bn = n // 8
    return pl.pallas_call(
        _ada_kernel,
        grid=(n // bn,),
        in_specs=[pl.BlockSpec((bsz, d), lambda j: (0, 0)),
                  pl.BlockSpec((d, bn), lambda j: (0, j)),
                  pl.BlockSpec((1, bn), lambda j: (0, j))],
        out_specs=pl.BlockSpec((bsz, bn), lambda j: (0, j)),
        out_shape=jax.ShapeDtypeStruct((bsz, n), F32),
        compiler_params=pltpu.CompilerParams(dimension_semantics=("arbitrary",)),
        name="ada_mod",
    )(c, w_ada, b_ada.reshape(1, n))


def _s5_prep_kernel(ar_ref, ai_ref, ldt_ref, br_ref, bi_ref,
                    abr_ref, abi_ref, bbr_ref, bbi_ref):
    ar, ai = ar_ref[...], ai_ref[...]
    dt = jnp.exp(ldt_ref[...])
    mag = jnp.exp(dt * ar)
    ab_re = mag * jnp.cos(dt * ai)
    ab_im = mag * jnp.sin(dt * ai)
    den = ar * ar + ai * ai
    nr, ni = ab_re - 1.0, ab_im
    f_re = (nr * ar + ni * ai) / den
    f_im = (ni * ar - nr * ai) / den
    br, bi = br_ref[...], bi_ref[...]
    abr_ref[...] = ab_re
    abi_ref[...] = ab_im
    bbr_ref[...] = f_re * br - f_im * bi
    bbi_ref[...] = f_re * bi + f_im * br


def _s5_prep_call(a_re, a_im, log_dt, b_re, b_im):
    g, p = a_re.shape
    h = b_re.shape[-1]
    n = g * p
    row = lambda v: v.reshape(1, n)
    to_hn = lambda v: v.reshape(n, h).T
    vec = jax.ShapeDtypeStruct((1, n), F32)
    mat = jax.ShapeDtypeStruct((h, n), F32)
    return pl.pallas_call(
        _s5_prep_kernel,
        out_shape=(vec, vec, mat, mat),
        name="s5_prep",
    )(row(a_re), row(a_im), row(jnp.repeat(log_dt, p)), to_hn(b_re), to_hn(b_im))


def _swiglu_ln(x2, h2, gate2, w1_ref, w3_ref, w2_ref, g_ref, b_ref, alpha):
    hb = h2.astype(BF16)
    acc = None
    for lo, hi in _ffn_chunks(w1_ref.shape[1]):
        a = _dot(hb, w1_ref[:, lo:hi])
        b = _dot(hb, w3_ref[:, lo:hi])
        gg = (_silu(a) * b).astype(BF16)
        part = _dot(gg, w2_ref[lo:hi, :])
        acc = part if acc is None else acc + part
    y = alpha * x2 + (0.5 * gate2) * acc
    return _layer_norm(y, g_ref[...], b_ref[...])


def _ffn1_kernel(x_ref, mod_ref, w1_ref, w3_ref, w2_ref, g_ref, b_ref, o_ref, tmp_ref, *, alpha):
    nb, tt, d = x_ref.shape
    sh, sc, gt = mod_ref[0], mod_ref[1], mod_ref[2]
    x3 = x_ref[...]
    h3 = x3 * (1.0 + sc)[:, None, :] + sh[:, None, :]
    gate3 = jnp.broadcast_to(gt[:, None, :], (nb, tt, d))
    out = _swiglu_ln(x3.reshape(nb * tt, d), h3.reshape(nb * tt, d),
                     gate3.reshape(nb * tt, d), w1_ref, w3_ref, w2_ref, g_ref, b_ref, alpha)
    for j in range(d // LANES):
        for b in range(nb):
            tmp_ref[j, pl.ds(b, tt, stride=nb), :] = out[b * tt:(b + 1) * tt, j * LANES:(j + 1) * LANES]
        o_ref[:, j * LANES:(j + 1) * LANES] = tmp_ref[j]


def _ffn2_kernel(x_ref, y_ref, mod2_ref, mod3_ref, wo_ref, g2_ref, b2_ref,
                 w1_ref, w3_ref, w2_ref, g3_ref, b3_ref, o_ref, tmp_ref, *, alpha):
    nb, tt, d = o_ref.shape
    x1 = x_ref[...].reshape(tt, nb, d)
    m = _dot(y_ref[...], wo_ref[...]).reshape(tt, nb, d)
    x2 = _layer_norm((alpha * x1 + mod2_ref[2][None] * m).reshape(tt * nb, d),
                     g2_ref[...], b2_ref[...])
    sh, sc, gt = mod3_ref[0], mod3_ref[1], mod3_ref[2]
    h3 = x2.reshape(tt, nb, d) * (1.0 + sc)[None] + sh[None]
    gate3 = jnp.broadcast_to(gt[None], (tt, nb, d))
    out = _swiglu_ln(x2, h3.reshape(tt * nb, d), gate3.reshape(tt * nb, d),
                     w1_ref, w3_ref, w2_ref, g3_ref, b3_ref, alpha)
    for j in range(d // LANES):
        tmp_ref[j] = out[:, j * LANES:(j + 1) * LANES]
        for b in range(nb):
            o_ref[b, :, j * LANES:(j + 1) * LANES] = tmp_ref[j, pl.ds(b, tt, stride=nb), :]


def _ffn_chunks(d_ff):
    assert d_ff % MXU_TILE == 0
    cut = (d_ff // MXU_TILE + 1) // 2 * MXU_TILE
    return ((0, cut), (cut, d_ff)) if cut < d_ff else ((0, d_ff),)


def _ffn_weights(w1, w3, w2):
    return w1.astype(BF16), w3.astype(BF16), w2.astype(BF16)


def _ffn1_call(x, mod1, w1, w3, w2, ln_g, ln_b, alpha):
    bsz, s, d = x.shape
    nh = bsz // NB
    tt = FFN_TT
    w1c, w3c, w2c = _ffn_weights(w1, w3, w2)
    row = lambda v: v.reshape(1, d)
    return pl.pallas_call(
        functools.partial(_ffn1_kernel, alpha=alpha),
        grid=(nh, s // tt),
        in_specs=[pl.BlockSpec((NB, tt, d), lambda h, i: (h, i, 0)),
                  pl.BlockSpec((None, 3, NB, d), lambda h, i: (h, 0, 0, 0)),
                  _resident(w1c.shape), _resident(w3c.shape), _resident(w2c.shape),
                  _resident((1, d)), _resident((1, d))],
        out_specs=pl.BlockSpec((None, tt * NB, d), lambda h, i: (h, i, 0)),
        out_shape=jax.ShapeDtypeStruct((nh, s * NB, d), F32),
        scratch_shapes=[pltpu.VMEM((d // LANES, tt * NB, LANES), F32)],
        compiler_params=pltpu.CompilerParams(
            dimension_semantics=("arbitrary", "arbitrary"), vmem_limit_bytes=VMEM_LIMIT_BYTES),
        name="ffn1_ln1",
    )(x, mod1, w1c, w3c, w2c, row(ln_g), row(ln_b))


def _ffn2_call(x1t, ycat, mod2, mod3, w_out, ln2_g, ln2_b, w1, w3, w2, ln3_g, ln3_b, alpha):
    nh, rows, d = x1t.shape
    s = rows // NB
    tt = FFN_TT
    w1c, w3c, w2c = _ffn_weights(w1, w3, w2)
    row = lambda v: v.reshape(1, d)
    return pl.pallas_call(
        functools.partial(_ffn2_kernel, alpha=alpha),
        grid=(nh, s // tt),
        in_specs=[pl.BlockSpec((None, tt * NB, d), lambda h, i: (h, i, 0)),
                  pl.BlockSpec((None, tt * NB, d), lambda h, i: (h, i, 0)),
                  pl.BlockSpec((None, 3, NB, d), lambda h, i: (h, 0, 0, 0)),
                  pl.BlockSpec((None, 3, NB, d), lambda h, i: (h, 0, 0, 0)),
                  _resident(w_out.shape), _resident((1, d)), _resident((1, d)),
                  _resident(w1c.shape), _resident(w3c.shape), _resident(w2c.shape),
                  _resident((1, d)), _resident((1, d))],
        out_specs=pl.BlockSpec((NB, tt, d), lambda h, i: (h, i, 0)),
        out_shape=jax.ShapeDtypeStruct((nh * NB, s, d), F32),
        scratch_shapes=[pltpu.VMEM((d // LANES, tt * NB, LANES), F32)],
        compiler_params=pltpu.CompilerParams(
            dimension_semantics=("arbitrary", "arbitrary"), vmem_limit_bytes=VMEM_LIMIT_BYTES),
        name="outproj_ln2_ffn2_ln3",
    )(x1t, ycat, mod2, mod3, w_out.astype(BF16), row(ln2_g), row(ln2_b),
      w1c, w3c, w2c, row(ln3_g), row(ln3_b))


def _mixer_kernel(x_ref, mod_ref, wz_ref, wx_ref, wu_ref, wdt_ref, cw_ref, cb_ref,
                  dtb_ref, alog_ref, dfull_ref, nw_ref, hexp3_ref,
                  abr_ref, abi_ref, bmat_ref, cmat_ref, s5d_ref, wg_ref, bg_ref,
                  o_ref,
                  z_buf, xbc_buf, act_buf, u_buf, acs_buf, y_buf, cdec_buf,
                  hstate, bu0_buf, bu1_buf, xb0_buf, xb1_buf, y5_buf, s5_state,
                  *, n_heads, n_groups):
    rows, d = x_ref.shape
    nb = NB
    tc = rows // nb
    halo = (SSD_CONV - 1) * nb
    ssd_w = z_buf.shape[1]
    conv_ch = xbc_buf.shape[0] * LANES
    xp = ssd_w // LANES
    assert SSD_STATE == LANES
    hpg = n_heads // n_groups
    gw = hpg * SSD_HEAD_DIM
    n_half, hk = bmat_ref.shape[0], bmat_ref.shape[1]
    sk = bmat_ref.shape[2] // 2
    s5_w = n_half * hk
    c_idx = pl.program_id(1)

    @pl.when(c_idx == 0)
    def _():
        xbc_buf[:, 0:halo, :] = jnp.zeros((conv_ch // LANES, halo, LANES), F32)
        hstate[...] = jnp.zeros(hstate.shape, F32)
        s5_state[...] = jnp.zeros(s5_state.shape, F32)

    sh, sc = mod_ref[0], mod_ref[1]
    hb = (x_ref[...].reshape(tc, nb, d) * (1.0 + sc)[None] + sh[None]).reshape(rows, d).astype(BF16)
    z_buf[...] = _dot(hb, wz_ref[...])
    xbc = _dot(hb, wx_ref[...])
    for j in range(conv_ch // LANES):
        xbc_buf[j, halo:halo + rows, :] = xbc[:, j * LANES:(j + 1) * LANES]
    for k in range(n_half):
        u_buf[k] = _dot(hb, wu_ref[:, k * hk:(k + 1) * hk])
    dt_raw = _dot(hb, wdt_ref[...])

    for j in range(conv_ch // LANES):
        cs = slice(j * LANES, (j + 1) * LANES)
        conv = cb_ref[:, cs]
        for k in range(SSD_CONV):
            conv = conv + cw_ref[k:k + 1, cs] * xbc_buf[j, k * nb:k * nb + rows, :]
        act_buf[j] = _silu(conv)
        xbc_buf[j, 0:halo, :] = xbc_buf[j, rows:rows + halo, :]

    dt = _softplus(dt_raw + dtb_ref[...])
    acs = dt * (-jnp.exp(alog_ref[...]))
    shift = nb
    while shift < rows:
        acs = acs + jnp.concatenate([jnp.zeros((shift, LANES), F32), acs[:rows - shift]], axis=0)
        shift *= 2
    acs_buf[...] = acs
    hi = acs.astype(BF16)
    r1 = acs - hi.astype(F32)
    mid = r1.astype(BF16)
    lo = (r1 - mid.astype(F32)).astype(BF16)
    acs_full = _dot(jnp.concatenate([hi, mid, lo], axis=-1), hexp3_ref[...])
    dt_full = _dot(dt.astype(BF16), hexp3_ref[0:LANES, :])
    last = acs_full[rows - nb:rows, :]
    cdec_buf[...] = jnp.exp(last)
    dec = jnp.exp(last[None] - acs_full.reshape(tc, nb, ssd_w)).reshape(rows, ssd_w)
    eac = jnp.exp(acs_full)
    for j in range(xp):
        cs = slice(j * LANES, (j + 1) * LANES)
        xs = act_buf[j]
        xin = xs * dt_full[:, cs]
        y_buf[j] = dfull_ref[:, cs] * xs
        act_buf[j] = xin
        xbc_buf[j, halo:halo + rows, :] = xin * dec[:, cs]
        xbc_buf[xp + j, halo:halo + rows, :] = eac[:, cs]

    causal = (lax.broadcasted_iota(jnp.int32, (tc, tc), 0)
              >= lax.broadcasted_iota(jnp.int32, (tc, tc), 1))
    gp = gw // LANES
    lane_head = lax.broadcasted_iota(jnp.int32, (tc, gw), 1) // SSD_HEAD_DIM

    def ssd_one(b, carry):
        rsel = pl.ds(b, tc, stride=nb)
        acs_b = acs_buf[rsel, :]
        acs_t = acs_b.T
        cdec = cdec_buf[pl.ds(b, 1), :]
        for g in range(n_groups):
            gather = lambda buf, p0, rs: jnp.concatenate(
                [buf[p0 + g * gp + jj, rs, :] for jj in range(gp)], axis=-1)
            rsel_h = pl.ds(halo + b, tc, stride=nb)
            xin_g = gather(act_buf, 0, rsel)
            xdec_g, eacs_g = gather(xbc_buf, 0, rsel_h), gather(xbc_buf, xp, rsel_h)
            bm = act_buf[xp + g, rsel, :]
            cm_b = act_buf[xp + n_groups + g, rsel, :].astype(BF16)
            cbm = lax.dot_general(cm_b, bm.astype(BF16), (((1,), (1,)), ((), ())),
                                  preferred_element_type=F32)
            h_prev = hstate[b, g]
            y_off = _dot(cm_b, h_prev.astype(BF16)) * eacs_g
            mms, xblk = [], []
            for z in range(hpg):
                hd = g * hpg + z
                seg = acs_b[:, hd:hd + 1] - acs_t[hd:hd + 1, :]
                lm = jnp.where(causal, jnp.exp(seg), 0.0)
                mms.append((cbm * lm).astype(BF16))
                xblk.append(jnp.where(lane_head == z, xin_g, 0.0).astype(BF16))
            y_g = _dot(jnp.concatenate(mms, axis=-1), jnp.concatenate(xblk, axis=0)) + y_off
            for jj in range(gp):
                y_buf[g * gp + jj, rsel, :] = (y_buf[g * gp + jj, rsel, :]
                                               + y_g[:, jj * LANES:(jj + 1) * LANES])
            st = _dot(bm.T.astype(BF16), xdec_g.astype(BF16))
            hstate[b, g] = cdec[:, g * gw:(g + 1) * gw] * h_prev + st
        return carry

    sub_rows = S5_SUB_T * nb
    n_units = (tc // S5_SUB_T) * n_half
    assert n_units == nb and n_half == 2

    bu_bufs, xb_bufs = (bu0_buf, bu1_buf), (xb0_buf, xb1_buf)

    def unit_rows(i):
        start = (i // n_half) * sub_rows
        return pl.ds(start if isinstance(i, int) else pl.multiple_of(start, sub_rows), sub_rows)

    def s5_in(i, k):
        bu_bufs[k][...] = _dot(u_buf[k, unit_rows(i), :].astype(BF16), bmat_ref[k])

    def s5_scan(k):
        bu, xb = bu_bufs[k], xb_bufs[k]
        a_r, a_i = abr_ref[k], abi_ref[k]
        xr, xi = s5_state[k, 0], s5_state[k, 1]
        for t in range(0, S5_SUB_T, 2):
            res_r, res_i = [], []
            for tt in (t, t + 1):
                rr = slice(tt * nb, (tt + 1) * nb)
                xr, xi = (a_r * xr - a_i * xi + bu[rr, 0:sk],
                          a_r * xi + a_i * xr + bu[rr, sk:2 * sk])
                res_r.append(xr)
                res_i.append(xi)
            r2 = slice(t * nb, (t + 2) * nb)
            xb[r2, 0:sk] = jnp.concatenate(res_r, axis=0).astype(BF16)
            xb[r2, sk:2 * sk] = jnp.concatenate(res_i, axis=0).astype(BF16)
        s5_state[k, 0] = xr
        s5_state[k, 1] = xi

    def s5_out(i, k):
        y5_buf[k, unit_rows(i), :] = _dot(xb_bufs[k][...], cmat_ref[k])

    def stage(i, k, first=False, last=False):
        ssd_one(i, 0)
        if not last:
            s5_in(i + 1, 1 - k)
        s5_scan(k)
        if not first:
            s5_out(i - 1, 1 - k)

    s5_in(0, 0)
    stage(0, 0, first=True)

    def loop_body(p, carry):
        stage(2 * p + 1, 1)
        stage(2 * p + 2, 0)
        return carry

    lax.fori_loop(0, (n_units - 2) // 2, loop_body, 0)
    stage(n_units - 1, 1, last=True)
    s5_out(n_units - 1, 1)

    y = jnp.concatenate([y_buf[j] for j in range(xp)], axis=-1)
    y = y * _silu(z_buf[...])
    nw = ssd_w // n_groups
    parts = []
    for g in range(n_groups):
        yg = y[:, g * nw:(g + 1) * nw]
        parts.append(yg * lax.rsqrt(jnp.mean(yg * yg, axis=-1, keepdims=True) + LN_EPS))
    o_ref[:, 0:ssd_w] = (jnp.concatenate(parts, axis=-1) * nw_ref[...]).astype(o_ref.dtype)

    y5 = jnp.concatenate([y5_buf[k] + u_buf[k] * s5d_ref[:, k * hk:(k + 1) * hk]
                          for k in range(n_half)], axis=-1)
    gl = jax.nn.gelu(y5, approximate=True)
    gate = _sigmoid(_dot(gl.astype(BF16), wg_ref[...]) + bg_ref[...])
    o_ref[:, ssd_w:ssd_w + s5_w] = (gl * gate).astype(o_ref.dtype)


def _mixer_call(x1t, mod2, w_in, conv_w, conv_b, dt_bias, a_log, d_ssd, ssd_norm_w,
                ab_re, ab_im, bmat, cmat, s5_d, w_glu, b_glu):
    nh, rows_total, d = x1t.shape
    s = rows_total // NB
    ssd_w = ssd_norm_w.shape[0]
    n_heads = d_ssd.shape[0]
    conv_ch = conv_w.shape[1]
    s5_w = s5_d.shape[0]
    n_groups = SSD_GROUPS
    tc = SSD_CHUNK
    rows = tc * NB
    halo = (SSD_CONV - 1) * NB

    o1, o2, o3 = ssd_w, ssd_w + conv_ch, ssd_w + conv_ch + n_heads
    w_z = w_in[:, :o1].astype(BF16)
    w_x = w_in[:, o1:o2].astype(BF16)
    w_dt = jnp.pad(w_in[:, o2:o3], ((0, 0), (0, LANES - n_heads))).astype(BF16)
    w_u = w_in[:, o3:].astype(BF16)
    pad_h = lambda v: jnp.pad(v, (0, LANES - n_heads)).reshape(1, LANES)
    hexp = (jnp.arange(LANES)[:, None] == (jnp.arange(ssd_w) // SSD_HEAD_DIM)[None, :]).astype(BF16)
    hexp = jnp.concatenate([hexp] * 3, axis=0)
    d_full = jnp.repeat(d_ssd, SSD_HEAD_DIM).reshape(1, ssd_w)

    n_half, hk, sk2 = bmat.shape
    kern = functools.partial(_mixer_kernel, n_heads=n_heads, n_groups=n_groups)
    xmap = lambda h, c: (h, c, 0)
    return pl.pallas_call(
        kern,
        grid=(nh, s // tc),
        in_specs=[pl.BlockSpec((None, rows, d), xmap),
                  pl.BlockSpec((None, 3, NB, d), lambda h, c: (h, 0, 0, 0)),
                  _resident(w_z.shape), _resident(w_x.shape), _resident(w_u.shape),
                  _resident(w_dt.shape),
                  _resident(conv_w.shape), _resident((1, conv_ch)),
                  _resident((1, LANES)), _resident((1, LANES)),
                  _resident((1, ssd_w)), _resident((1, ssd_w)), _resident(hexp.shape),
                  _resident(ab_re.shape), _resident(ab_im.shape),
                  _resident(bmat.shape), _resident(cmat.shape),
                  _resident((1, s5_w)), _resident(w_glu.shape), _resident((1, s5_w))],
        out_specs=pl.BlockSpec((None, rows, ssd_w + s5_w), xmap),
        out_shape=jax.ShapeDtypeStruct((nh, rows_total, ssd_w + s5_w), BF16),
        scratch_shapes=[
            pltpu.VMEM((rows, ssd_w), F32),
            pltpu.VMEM((conv_ch // LANES, rows + halo, LANES), F32),
            pltpu.VMEM((conv_ch // LANES, rows, LANES), F32),
            pltpu.VMEM((n_half, rows, hk), F32),
            pltpu.VMEM((rows, LANES), F32),
            pltpu.VMEM((ssd_w // LANES, rows, LANES), F32),
            pltpu.VMEM((NB, ssd_w), F32),
            pltpu.VMEM((NB, n_groups, SSD_STATE, ssd_w // n_groups), F32),
            pltpu.VMEM((S5_SUB_T * NB, sk2), F32),
            pltpu.VMEM((S5_SUB_T * NB, sk2), F32),
            pltpu.VMEM((S5_SUB_T * NB, sk2), BF16),
            pltpu.VMEM((S5_SUB_T * NB, sk2), BF16),
            pltpu.VMEM((n_half, rows, hk), F32),
            pltpu.VMEM((n_half, 2, NB, sk2 // 2), F32),
        ],
        compiler_params=pltpu.CompilerParams(
            dimension_semantics=("arbitrary", "arbitrary"), vmem_limit_bytes=VMEM_LIMIT_BYTES),
        name="mixer",
    )(x1t, mod2, w_z, w_x, w_u, w_dt, conv_w, conv_b.reshape(1, conv_ch),
      pad_h(dt_bias), pad_h(a_log), d_full, ssd_norm_w.reshape(1, ssd_w), hexp,
      ab_re, ab_im, bmat, cmat, s5_d.reshape(1, s5_w), w_glu.astype(BF16),
      b_glu.reshape(1, s5_w))


def _s5_matrices(ab_re, ab_im, bb_re, bb_im, c_re, c_im):
    h, n = bb_re.shape
    g, _, p = c_re.shape
    n_half = 2
    gh = g // n_half
    eye = jnp.eye(gh, dtype=F32)

    def b_block(bb):
        v = bb.reshape(h, n_half, gh, p)
        return jnp.einsum("hkgp,gG->kghGp", v, eye).reshape(n_half, gh * h, gh * p)

    def c_block(c):
        v = c.reshape(n_half, gh, h, p)
        return jnp.einsum("kghp,gG->kgpGh", v, eye).reshape(n_half, gh * p, gh * h)

    bmat = jnp.concatenate([b_block(bb_re), b_block(bb_im)], axis=-1).astype(BF16)
    cmat = jnp.concatenate([c_block(c_re), c_block(-c_im)], axis=1).astype(BF16)
    bc = lambda v: jnp.broadcast_to(v.reshape(n_half, 1, n // n_half), (n_half, NB, n // n_half))
    return bc(ab_re), bc(ab_im), bmat, cmat


def kernel(x, c, w_ada, b_ada, ffn1_w1, ffn1_w3, ffn1_w2, ln1_g, ln1_b, w_in, conv_w, conv_b,
           dt_bias, a_log, d_ssd, ssd_norm_w, s5_a_re, s5_a_im, s5_log_dt, s5_b_re, s5_b_im,
           s5_c_re, s5_c_im, s5_d, w_glu, b_glu, w_out, ln2_g, ln2_b, ffn2_w1, ffn2_w3, ffn2_w2,
           ln3_g, ln3_b):
    bsz, s, d = x.shape
    depth = w_ada.shape[0]
    alpha = (2 * depth) ** 0.25
    assert depth == 1
    assert bsz % NB == 0 and s % SSD_CHUNK == 0 and s % FFN_TT == 0
    nh = bsz // NB
    for l in range(depth):
        mod = _ada_call(c, w_ada[l], b_ada[l]).reshape(bsz, N_MOD // 3, 3, d)
        mod = mod.reshape(nh, NB, N_MOD // 3, 3, d).transpose(2, 0, 3, 1, 4)
        ab_re, ab_im, bb_re, bb_im = _s5_prep_call(s5_a_re[l], s5_a_im[l], s5_log_dt[l],
                                                   s5_b_re[l], s5_b_im[l])
        ab_re, ab_im, bmat, cmat = _s5_matrices(ab_re, ab_im, bb_re, bb_im, s5_c_re[l], s5_c_im[l])
        x1t = _ffn1_call(x, mod[0], ffn1_w1[l], ffn1_w3[l], ffn1_w2[l], ln1_g[l], ln1_b[l], alpha)
        ycat = _mixer_call(x1t, mod[1], w_in[l], conv_w[l], conv_b[l], dt_bias[l], a_log[l],
                           d_ssd[l], ssd_norm_w[l], ab_re, ab_im, bmat, cmat, s5_d[l],
                           w_glu[l], b_glu[l])
        x = _ffn2_call(x1t, ycat, mod[1], mod[2], w_out[l], ln2_g[l], ln2_b[l],
                       ffn2_w1[l], ffn2_w3[l], ffn2_w2[l], ln3_g[l], ln3_b[l], alpha)
    return x
```

```python
import functools
import math

import jax
import jax.numpy as jnp
from jax import lax
from jax.experimental import pallas as pl
from jax.experimental.pallas import tpu as pltpu

F32 = jnp.float32
BF16 = jnp.bfloat16

LN_EPS = 1e-5
SUBLANES = 8
LANES = 128
MXU_TILE = 256
VMEM_LIMIT_BYTES = 56 * 1024 * 1024

SSD_HEAD_DIM = 64
SSD_GROUPS = 2
SSD_STATE = 128
SSD_CONV = 4
SSD_CHUNK = 128
S5_GROUP_CH = 16
S5_STATE = 64
N_MOD = 9

NB = SUBLANES
FFN_TT = 64
S5_SUB_T = 32


def _dot(a, b):
    return jnp.dot(a, b, preferred_element_type=F32)


def _sigmoid(x):
    return 1.0 / (1.0 + jnp.exp(-x))


def _silu(x):
    return x * _sigmoid(x)


def _softplus(x):
    return jnp.maximum(x, 0.0) + jnp.log1p(jnp.exp(-jnp.abs(x)))


def _layer_norm(y, g, b):
    mu = jnp.mean(y, axis=-1, keepdims=True)
    yc = y - mu
    var = jnp.mean(yc * yc, axis=-1, keepdims=True)
    return yc * lax.rsqrt(var + LN_EPS) * g + b


def _resident(shape):
    nd = len(shape)
    return pl.BlockSpec(shape, lambda *_: (0,) * nd, pipeline_mode=pl.Buffered(1))


def _ada_kernel(c_ref, w_ref, b_ref, o_ref):
    cs = _silu(c_ref[...]).astype(BF16)
    o_ref[...] = _dot(cs, w_ref[...].astype(BF16)) + b_ref[...]


def _ada_call(c, w_ada, b_ada):
    bsz, d = c.shape
    n = w_ada.shape[1]
    bn = n // 8
    return pl.pallas_call(
        _ada_kernel,
        grid=(n // bn,),
        in_specs=[pl.BlockSpec((bsz, d), lambda j: (0, 0)),
                  pl.BlockSpec((d, bn), lambda j: (0, j)),
                  pl.BlockSpec((1, bn), lambda j: (0, j))],
        out_specs=pl.BlockSpec((bsz, bn), lambda j: (0, j)),
        out_shape=jax.ShapeDtypeStruct((bsz, n), F32),
        compiler_params=pltpu.CompilerParams(dimension_semantics=("arbitrary",)),
        name="ada_mod",
    )(c, w_ada, b_ada.reshape(1, n))


def _s5_prep_kernel(ar_ref, ai_ref, ldt_ref, br_ref, bi_ref,
                    abr_ref, abi_ref, bbr_ref, bbi_ref):
    ar, ai = ar_ref[...], ai_ref[...]
    dt = jnp.exp(ldt_ref[...])
    mag = jnp.exp(dt * ar)
    ab_re = mag * jnp.cos(dt * ai)
    ab_im = mag * jnp.sin(dt * ai)
    den = ar * ar + ai * ai
    nr, ni = ab_re - 1.0, ab_im
    f_re = (nr * ar + ni * ai) / den
    f_im = (ni * ar - nr * ai) / den
    br, bi = br_ref[...], bi_ref[...]
    abr_ref[...] = ab_re
    abi_ref[...] = ab_im
    bbr_ref[...] = f_re * br - f_im * bi
    bbi_ref[...] = f_re * bi + f_im * br


def _s5_prep_call(a_re, a_im, log_dt, b_re, b_im):
    g, p = a_re.shape
    h = b_re.shape[-1]
    n = g * p
    row = lambda v: v.reshape(1, n)
    to_hn = lambda v: v.reshape(n, h).T
    vec = jax.ShapeDtypeStruct((1, n), F32)
    mat = jax.ShapeDtypeStruct((h, n), F32)
    return pl.pallas_call(
        _s5_prep_kernel,
        out_shape=(vec, vec, mat, mat),
        name="s5_prep",
    )(row(a_re), row(a_im), row(jnp.repeat(log_dt, p)), to_hn(b_re), to_hn(b_im))


def _swiglu_residual(x2, h2, gate2, w1_ref, w3_ref, w2_ref, alpha):
    hb = h2.astype(BF16)
    acc = None
    for lo, hi in _ffn_chunks(w1_ref.shape[1]):
        a = _dot(hb, w1_ref[:, lo:hi])
        b = _dot(hb, w3_ref[:, lo:hi])
        gg = (_silu(a) * b).astype(BF16)
        part = _dot(gg, w2_ref[lo:hi, :])
        acc = part if acc is None else acc + part
    return alpha * x2 + (0.5 * gate2) * acc


def _delayed_epilogue(step, n_tiles, compute, finish):
    @pl.when(step == 0)
    def _():
        compute()

    @pl.when(jnp.logical_and(step > 0, step < n_tiles))
    def _():
        finish()
        compute()

    @pl.when(step == n_tiles)
    def _():
        finish()


def _ffn1_kernel(x_ref, mod_ref, w1_ref, w3_ref, w2_ref, g_ref, b_ref, o_ref, pend_ref, tmp_ref,
                 *, alpha, n_tiles):
    nb, tt, d = x_ref.shape

    def compute():
        sh, sc, gt = mod_ref[0], mod_ref[1], mod_ref[2]
        x3 = x_ref[...]
        h3 = x3 * (1.0 + sc)[:, None, :] + sh[:, None, :]
        gate3 = jnp.broadcast_to(gt[:, None, :], (nb, tt, d))
        pend_ref[...] = _swiglu_residual(x3.reshape(nb * tt, d), h3.reshape(nb * tt, d),
                                         gate3.reshape(nb * tt, d), w1_ref, w3_ref, w2_ref, alpha)

    def finish():
        out = _layer_norm(pend_ref[...], g_ref[...], b_ref[...])
        for j in range(d // LANES):
            for b in range(nb):
                tmp_ref[j, pl.ds(b, tt, stride=nb), :] = out[b * tt:(b + 1) * tt,
                                                             j * LANES:(j + 1) * LANES]
            o_ref[:, j * LANES:(j + 1) * LANES] = tmp_ref[j]

    _delayed_epilogue(pl.program_id(0), n_tiles, compute, finish)


def _ffn2_kernel(x_ref, y_ref, mod2_ref, mod3_ref, wo_ref, g2_ref, b2_ref,
                 w1_ref, w3_ref, w2_ref, g3_ref, b3_ref, o_ref, pend_ref, tmp_ref,
                 *, alpha, n_tiles):
    nb, tt, d = o_ref.shape

    def compute():
        x1 = x_ref[...].reshape(tt, nb, d)
        m = _dot(y_ref[...], wo_ref[...]).reshape(tt, nb, d)
        x2 = _layer_norm((alpha * x1 + mod2_ref[2][None] * m).reshape(tt * nb, d),
                         g2_ref[...], b2_ref[...])
        sh, sc, gt = mod3_ref[0], mod3_ref[1], mod3_ref[2]
        h3 = x2.reshape(tt, nb, d) * (1.0 + sc)[None] + sh[None]
        gate3 = jnp.broadcast_to(gt[None], (tt, nb, d))
        pend_ref[...] = _swiglu_residual(x2, h3.reshape(tt * nb, d), gate3.reshape(tt * nb, d),
                                         w1_ref, w3_ref, w2_ref, alpha)

    def finish():
        out = _layer_norm(pend_ref[...], g3_ref[...], b3_ref[...])
        for j in range(d // LANES):
            tmp_ref[j] = out[:, j * LANES:(j + 1) * LANES]
            for b in range(nb):
                o_ref[b, :, j * LANES:(j + 1) * LANES] = tmp_ref[j, pl.ds(b, tt, stride=nb), :]

    _delayed_epilogue(pl.program_id(0), n_tiles, compute, finish)


def _ffn_chunks(d_ff):
    assert d_ff % MXU_TILE == 0
    cut = (d_ff // MXU_TILE + 1) // 2 * MXU_TILE
    return ((0, cut), (cut, d_ff)) if cut < d_ff else ((0, d_ff),)


def _ffn_weights(w1, w3, w2):
    return w1.astype(BF16), w3.astype(BF16), w2.astype(BF16)


def _tile_maps(n_tiles, per_half):
    def split(t):
        return t // per_half, t % per_half
    in_tile = lambda s: split(jnp.minimum(s, n_tiles - 1))
    out_tile = lambda s: split(jnp.maximum(s - 1, 0))
    return in_tile, out_tile


def _ffn1_call(x, mod1, w1, w3, w2, ln_g, ln_b, alpha):
    bsz, s, d = x.shape
    nh = bsz // NB
    tt = FFN_TT
    per_half = s // tt
    n_tiles = nh * per_half
    in_tile, out_tile = _tile_maps(n_tiles, per_half)
    w1c, w3c, w2c = _ffn_weights(w1, w3, w2)
    row = lambda v: v.reshape(1, d)
    return pl.pallas_call(
        functools.partial(_ffn1_kernel, alpha=alpha, n_tiles=n_tiles),
        grid=(n_tiles + 1,),
        in_specs=[pl.BlockSpec((NB, tt, d), lambda s: (*in_tile(s), 0)),
                  pl.BlockSpec((None, 3, NB, d), lambda s: (in_tile(s)[0], 0, 0, 0)),
                  _resident(w1c.shape), _resident(w3c.shape), _resident(w2c.shape),
                  _resident((1, d)), _resident((1, d))],
        out_specs=pl.BlockSpec((None, tt * NB, d), lambda s: (*out_tile(s), 0)),
        out_shape=jax.ShapeDtypeStruct((nh, s * NB, d), F32),
        scratch_shapes=[pltpu.VMEM((tt * NB, d), F32),
                        pltpu.VMEM((d // LANES, tt * NB, LANES), F32)],
        compiler_params=pltpu.CompilerParams(
            dimension_semantics=("arbitrary",), vmem_limit_bytes=VMEM_LIMIT_BYTES),
        name="ffn1_ln1",
    )(x, mod1, w1c, w3c, w2c, row(ln_g), row(ln_b))


def _ffn2_call(x1t, ycat, mod2, mod3, w_out, ln2_g, ln2_b, w1, w3, w2, ln3_g, ln3_b, alpha):
    nh, rows, d = x1t.shape
    s = rows // NB
    tt = FFN_TT
    per_half = s // tt
    n_tiles = nh * per_half
    in_tile, out_tile = _tile_maps(n_tiles, per_half)
    w1c, w3c, w2c = _ffn_weights(w1, w3, w2)
    row = lambda v: v.reshape(1, d)
    tile_in = pl.BlockSpec((None, tt * NB, d), lambda s: (*in_tile(s), 0))
    mod_in = pl.BlockSpec((None, 3, NB, d), lambda s: (in_tile(s)[0], 0, 0, 0))
    return pl.pallas_call(
        functools.partial(_ffn2_kernel, alpha=alpha, n_tiles=n_tiles),
        grid=(n_tiles + 1,),
        in_specs=[tile_in, tile_in, mod_in, mod_in,
                  _resident(w_out.shape), _resident((1, d)), _resident((1, d)),
                  _resident(w1c.shape), _resident(w3c.shape), _resident(w2c.shape),
                  _resident((1, d)), _resident((1, d))],
        out_specs=pl.BlockSpec((NB, tt, d), lambda s: (*out_tile(s), 0)),
        out_shape=jax.ShapeDtypeStruct((nh * NB, s, d), F32),
        scratch_shapes=[pltpu.VMEM((tt * NB, d), F32),
                        pltpu.VMEM((d // LANES, tt * NB, LANES), F32)],
        compiler_params=pltpu.CompilerParams(
            dimension_semantics=("arbitrary",), vmem_limit_bytes=VMEM_LIMIT_BYTES),
        name="outproj_ln2_ffn2_ln3",
    )(x1t, ycat, mod2, mod3, w_out.astype(BF16), row(ln2_g), row(ln2_b),
      w1c, w3c, w2c, row(ln3_g), row(ln3_b))


def _mixer_kernel(x_ref, mod_ref, wz_ref, wx_ref, wu_ref, wdt_ref, cw_ref, cb_ref,
                  dtb_ref, alog_ref, dfull_ref, nw_ref, hexp3_ref,
                  abr_ref, abi_ref, bmat_ref, cmat_ref, s5d_ref, wg_ref, bg_ref,
                  o_ref,
                  z_buf, xbc_buf, act_buf, u_buf, acs_buf, y_buf, cdec_buf,
                  hstate, bu0_buf, bu1_buf, xb0_buf, xb1_buf, y5_buf, s5_state,
                  *, n_heads, n_groups):
    rows, d = x_ref.shape
    nb = NB
    tc = rows // nb
    halo = (SSD_CONV - 1) * nb
    ssd_w = z_buf.shape[1]
    conv_ch = xbc_buf.shape[0] * LANES
    xp = ssd_w // LANES
    assert SSD_STATE == LANES
    hpg = n_heads // n_groups
    gw = hpg * SSD_HEAD_DIM
    n_half, hk = bmat_ref.shape[0], bmat_ref.shape[1]
    sk = bmat_ref.shape[2] // 2
    s5_w = n_half * hk
    c_idx = pl.program_id(1)

    @pl.when(c_idx == 0)
    def _():
        xbc_buf[:, 0:halo, :] = jnp.zeros((conv_ch // LANES, halo, LANES), F32)
        hstate[...] = jnp.zeros(hstate.shape, F32)
        s5_state[...] = jnp.zeros(s5_state.shape, F32)

    sh, sc = mod_ref[0], mod_ref[1]
    hb = (x_ref[...].reshape(tc, nb, d) * (1.0 + sc)[None] + sh[None]).reshape(rows, d).astype(BF16)
    xbc = _dot(hb, wx_ref[...])
    for j in range(conv_ch // LANES):
        xbc_buf[j, halo:halo + rows, :] = xbc[:, j * LANES:(j + 1) * LANES]
    dt_raw = _dot(hb, wdt_ref[...])
    for k in range(n_half):
        u_buf[k] = _dot(hb, wu_ref[:, k * hk:(k + 1) * hk])
    z_buf[...] = _dot(hb, wz_ref[...])

    for j in range(conv_ch // LANES):
        cs = slice(j * LANES, (j + 1) * LANES)
        conv = cb_ref[:, cs]
        for k in range(SSD_CONV):
            conv = conv + cw_ref[k:k + 1, cs] * xbc_buf[j, k * nb:k * nb + rows, :]
        act_buf[j] = _silu(conv)
        xbc_buf[j, 0:halo, :] = xbc_buf[j, rows:rows + halo, :]

    dt = _softplus(dt_raw + dtb_ref[...])
    acs = dt * (-jnp.exp(alog_ref[...]))
    shift = nb
    while shift < rows:
        acs = acs + jnp.concatenate([jnp.zeros((shift, LANES), F32), acs[:rows - shift]], axis=0)
        shift *= 2
    acs_buf[...] = acs
    hi = acs.astype(BF16)
    r1 = acs - hi.astype(F32)
    mid = r1.astype(BF16)
    lo = (r1 - mid.astype(F32)).astype(BF16)
    acs_full = _dot(jnp.concatenate([hi, mid, lo], axis=-1), hexp3_ref[...])
    dt_full = _dot(dt.astype(BF16), hexp3_ref[0:LANES, :])
    last = acs_full[rows - nb:rows, :]
    cdec_buf[...] = jnp.exp(last)
    dec = jnp.exp(last[None] - acs_full.reshape(tc, nb, ssd_w)).reshape(rows, ssd_w)
    eac = jnp.exp(acs_full)
    for j in range(xp):
        cs = slice(j * LANES, (j + 1) * LANES)
        xs = act_buf[j]
        xin = xs * dt_full[:, cs]
        y_buf[j] = dfull_ref[:, cs] * xs
        act_buf[j] = xin
        xbc_buf[j, halo:halo + rows, :] = xin * dec[:, cs]
        xbc_buf[xp + j, halo:halo + rows, :] = eac[:, cs]

    causal = (lax.broadcasted_iota(jnp.int32, (tc, tc), 0)
              >= lax.broadcasted_iota(jnp.int32, (tc, tc), 1))
    gp = gw // LANES
    lane_head = lax.broadcasted_iota(jnp.int32, (tc, gw), 1) // SSD_HEAD_DIM

    def ssd_pre(b):
        rsel = pl.ds(b, tc, stride=nb)
        acs_b = acs_buf[rsel, :]
        return rsel, acs_b, acs_b.T, cdec_buf[pl.ds(b, 1), :]

    def ssd_group(b, g, pre):
        rsel, acs_b, acs_t, cdec = pre
        gather = lambda buf, p0, rs: jnp.concatenate(
            [buf[p0 + g * gp + jj, rs, :] for jj in range(gp)], axis=-1)
        rsel_h = pl.ds(halo + b, tc, stride=nb)
        xin_g = gather(act_buf, 0, rsel)
        xdec_g, eacs_g = gather(xbc_buf, 0, rsel_h), gather(xbc_buf, xp, rsel_h)
        bm = act_buf[xp + g, rsel, :]
        cm_b = act_buf[xp + n_groups + g, rsel, :].astype(BF16)
        cbm = lax.dot_general(cm_b, bm.astype(BF16), (((1,), (1,)), ((), ())),
                              preferred_element_type=F32)
        h_prev = hstate[b, g]
        y_off = _dot(cm_b, h_prev.astype(BF16)) * eacs_g
        mms, xblk = [], []
        for z in range(hpg):
            hd = g * hpg + z
            seg = acs_b[:, hd:hd + 1] - acs_t[hd:hd + 1, :]
            lm = jnp.where(causal, jnp.exp(seg), 0.0)
            mms.append((cbm * lm).astype(BF16))
            xblk.append(jnp.where(lane_head == z, xin_g, 0.0).astype(BF16))
        y_g = _dot(jnp.concatenate(mms, axis=-1), jnp.concatenate(xblk, axis=0)) + y_off
        for jj in range(gp):
            y_buf[g * gp + jj, rsel, :] = (y_buf[g * gp + jj, rsel, :]
                                           + y_g[:, jj * LANES:(jj + 1) * LANES])
        st = _dot(bm.T.astype(BF16), xdec_g.astype(BF16))
        hstate[b, g] = cdec[:, g * gw:(g + 1) * gw] * h_prev + st

    sub_rows = S5_SUB_T * nb
    n_units = (tc // S5_SUB_T) * n_half
    assert n_units == nb and n_half == 2

    bu_bufs, xb_bufs = (bu0_buf, bu1_buf), (xb0_buf, xb1_buf)

    def unit_rows(i):
        start = (i // n_half) * sub_rows
        return pl.ds(start if isinstance(i, int) else pl.multiple_of(start, sub_rows), sub_rows)

    def s5_in(i, k):
        bu_bufs[k][...] = _dot(u_buf[k, unit_rows(i), :].astype(BF16), bmat_ref[k])

    def s5_scan(k, t0, t1, xc):
        bu, xb = bu_bufs[k], xb_bufs[k]
        a_r, a_i = abr_ref[k], abi_ref[k]
        xr, xi = (s5_state[k, 0], s5_state[k, 1]) if xc is None else xc
        for t in range(t0, t1, 2):
            res_r, res_i = [], []
            for tt in (t, t + 1):
                rr = slice(tt * nb, (tt + 1) * nb)
                xr, xi = (a_r * xr - a_i * xi + bu[rr, 0:sk],
                          a_r * xi + a_i * xr + bu[rr, sk:2 * sk])
                res_r.append(xr)
                res_i.append(xi)
            r2 = slice(t * nb, (t + 2) * nb)
            xb[r2, 0:sk] = jnp.concatenate(res_r, axis=0).astype(BF16)
            xb[r2, sk:2 * sk] = jnp.concatenate(res_i, axis=0).astype(BF16)
        if t1 == S5_SUB_T:
            s5_state[k, 0] = xr
            s5_state[k, 1] = xi
        return xr, xi

    def s5_out(i, k):
        y5_buf[k, unit_rows(i), :] = _dot(xb_bufs[k][...], cmat_ref[k])

    def stage(i, k, first=False, last=False):
        s5_scan(k, 0, S5_SUB_T, None)
        if not last:
            s5_in(i + 1, 1 - k)
        if not first:
            s5_out(i - 1, 1 - k)
        pre = ssd_pre(i)
        for g in range(n_groups):
            ssd_group(i, g, pre)

    s5_in(0, 0)
    stage(0, 0, first=True)

    def loop_body(p, carry):
        stage(2 * p + 1, 1)
        stage(2 * p + 2, 0)
        return carry

    lax.fori_loop(0, (n_units - 2) // 2, loop_body, 0)
    stage(n_units - 1, 1, last=True)
    s5_out(n_units - 1, 1)

    y = jnp.concatenate([y_buf[j] for j in range(xp)], axis=-1)
    y = y * _silu(z_buf[...])
    nw = ssd_w // n_groups
    parts = []
    for g in range(n_groups):
        yg = y[:, g * nw:(g + 1) * nw]
        parts.append(yg * lax.rsqrt(jnp.mean(yg * yg, axis=-1, keepdims=True) + LN_EPS))
    o_ref[:, 0:ssd_w] = (jnp.concatenate(parts, axis=-1) * nw_ref[...]).astype(o_ref.dtype)

    y5 = jnp.concatenate([y5_buf[k] + u_buf[k] * s5d_ref[:, k * hk:(k + 1) * hk]
                          for k in range(n_half)], axis=-1)
    gl = jax.nn.gelu(y5, approximate=True)
    gate = _sigmoid(_dot(gl.astype(BF16), wg_ref[...]) + bg_ref[...])
    o_ref[:, ssd_w:ssd_w + s5_w] = (gl * gate).astype(o_ref.dtype)


def _mixer_call(x1t, mod2, w_in, conv_w, conv_b, dt_bias, a_log, d_ssd, ssd_norm_w,
                ab_re, ab_im, bmat, cmat, s5_d, w_glu, b_glu):
    nh, rows_total, d = x1t.shape
    s = rows_total // NB
    ssd_w = ssd_norm_w.shape[0]
    n_heads = d_ssd.shape[0]
    conv_ch = conv_w.shape[1]
    s5_w = s5_d.shape[0]
    n_groups = SSD_GROUPS
    tc = SSD_CHUNK
    rows = tc * NB
    halo = (SSD_CONV - 1) * NB

    o1, o2, o3 = ssd_w, ssd_w + conv_ch, ssd_w + conv_ch + n_heads
    w_z = w_in[:, :o1].astype(BF16)
    w_x = w_in[:, o1:o2].astype(BF16)
    w_dt = jnp.pad(w_in[:, o2:o3], ((0, 0), (0, LANES - n_heads))).astype(BF16)
    w_u = w_in[:, o3:].astype(BF16)
    pad_h = lambda v: jnp.pad(v, (0, LANES - n_heads)).reshape(1, LANES)
    hexp = (jnp.arange(LANES)[:, None] == (jnp.arange(ssd_w) // SSD_HEAD_DIM)[None, :]).astype(BF16)
    hexp = jnp.concatenate([hexp] * 3, axis=0)
    d_full = jnp.repeat(d_ssd, SSD_HEAD_DIM).reshape(1, ssd_w)

    n_half, hk, sk2 = bmat.shape
    kern = functools.partial(_mixer_kernel, n_heads=n_heads, n_groups=n_groups)
    xmap = lambda h, c: (h, c, 0)
    return pl.pallas_call(
        kern,
        grid=(nh, s // tc),
        in_specs=[pl.BlockSpec((None, rows, d), xmap),
                  pl.BlockSpec((None, 3, NB, d), lambda h, c: (h, 0, 0, 0)),
                  _resident(w_z.shape), _resident(w_x.shape), _resident(w_u.shape),
                  _resident(w_dt.shape),
                  _resident(conv_w.shape), _resident((1, conv_ch)),
                  _resident((1, LANES)), _resident((1, LANES)),
                  _resident((1, ssd_w)), _resident((1, ssd_w)), _resident(hexp.shape),
                  _resident(ab_re.shape), _resident(ab_im.shape),
                  _resident(bmat.shape), _resident(cmat.shape),
                  _resident((1, s5_w)), _resident(w_glu.shape), _resident((1, s5_w))],
        out_specs=pl.BlockSpec((None, rows, ssd_w + s5_w), xmap),
        out_shape=jax.ShapeDtypeStruct((nh, rows_total, ssd_w + s5_w), BF16),
        scratch_shapes=[
            pltpu.VMEM((rows, ssd_w), F32),
            pltpu.VMEM((conv_ch // LANES, rows + halo, LANES), F32),
            pltpu.VMEM((conv_ch // LANES, rows, LANES), F32),
            pltpu.VMEM((n_half, rows, hk), F32),
            pltpu.VMEM((rows, LANES), F32),
            pltpu.VMEM((ssd_w // LANES, rows, LANES), F32),
            pltpu.VMEM((NB, ssd_w), F32),
            pltpu.VMEM((NB, n_groups, SSD_STATE, ssd_w // n_groups), F32),
            pltpu.VMEM((S5_SUB_T * NB, sk2), F32),
            pltpu.VMEM((S5_SUB_T * NB, sk2), F32),
            pltpu.VMEM((S5_SUB_T * NB, sk2), BF16),
            pltpu.VMEM((S5_SUB_T * NB, sk2), BF16),
            pltpu.VMEM((n_half, rows, hk), F32),
            pltpu.VMEM((n_half, 2, NB, sk2 // 2), F32),
        ],
        compiler_params=pltpu.CompilerParams(
            dimension_semantics=("arbitrary", "arbitrary"), vmem_limit_bytes=VMEM_LIMIT_BYTES),
        name="mixer",
    )(x1t, mod2, w_z, w_x, w_u, w_dt, conv_w, conv_b.reshape(1, conv_ch),
      pad_h(dt_bias), pad_h(a_log), d_full, ssd_norm_w.reshape(1, ssd_w), hexp,
      ab_re, ab_im, bmat, cmat, s5_d.reshape(1, s5_w), w_glu.astype(BF16),
      b_glu.reshape(1, s5_w))


def _s5_matrices(ab_re, ab_im, bb_re, bb_im, c_re, c_im):
    h, n = bb_re.shape
    g, _, p = c_re.shape
    n_half = 2
    gh = g // n_half
    eye = jnp.eye(gh, dtype=F32)

    def b_block(bb):
        v = bb.reshape(h, n_half, gh, p)
        return jnp.einsum("hkgp,gG->kghGp", v, eye).reshape(n_half, gh * h, gh * p)

    def c_block(c):
        v = c.reshape(n_half, gh, h, p)
        return jnp.einsum("kghp,gG->kgpGh", v, eye).reshape(n_half, gh * p, gh * h)

    bmat = jnp.concatenate([b_block(bb_re), b_block(bb_im)], axis=-1).astype(BF16)
    cmat = jnp.concatenate([c_block(c_re), c_block(-c_im)], axis=1).astype(BF16)
    bc = lambda v: jnp.broadcast_to(v.reshape(n_half, 1, n // n_half), (n_half, NB, n // n_half))
    return bc(ab_re), bc(ab_im), bmat, cmat


def kernel(x, c, w_ada, b_ada, ffn1_w1, ffn1_w3, ffn1_w2, ln1_g, ln1_b, w_in, conv_w, conv_b,
           dt_bias, a_log, d_ssd, ssd_norm_w, s5_a_re, s5_a_im, s5_log_dt, s5_b_re, s5_b_im,
           s5_c_re, s5_c_im, s5_d, w_glu, b_glu, w_out, ln2_g, ln2_b, ffn2_w1, ffn2_w3, ffn2_w2,
           ln3_g, ln3_b):
    bsz, s, d = x.shape
    depth = w_ada.shape[0]
    alpha = (2 * depth) ** 0.25
    assert depth == 1
    assert bsz % NB == 0 and s % SSD_CHUNK == 0 and s % FFN_TT == 0
    nh = bsz // NB
    for l in range(depth):
        mod = _ada_call(c, w_ada[l], b_ada[l]).reshape(bsz, N_MOD // 3, 3, d)
        mod = mod.reshape(nh, NB, N_MOD // 3, 3, d).transpose(2, 0, 3, 1, 4)
        ab_re, ab_im, bb_re, bb_im = _s5_prep_call(s5_a_re[l], s5_a_im[l], s5_log_dt[l],
                                                   s5_b_re[l], s5_b_im[l])
        ab_re, ab_im, bmat, cmat = _s5_matrices(ab_re, ab_im, bb_re, bb_im, s5_c_re[l], s5_c_im[l])
        x1t = _ffn1_call(x, mod[0], ffn1_w1[l], ffn1_w3[l], ffn1_w2[l], ln1_g[l], ln1_b[l], alpha)
        ycat = _mixer_call(x1t, mod[1], w_in[l], conv_w[l], conv_b[l], dt_bias[l], a_log[l],
                           d_ssd[l], ssd_norm_w[l], ab_re, ab_im, bmat, cmat, s5_d[l],
                           w_glu[l], b_glu[l])
        x = _ffn2_call(x1t, ycat, mod[1], mod[2], w_out[l], ln2_g[l], ln2_b[l],
                       ffn2_w1[l], ffn2_w3[l], ffn2_w2[l], ln3_g[l], ln3_b[l], alpha)
    return x
```

```python
import functools

import jax
import jax.numpy as jnp
from jax import lax
from jax.experimental import pallas as pl
from jax.experimental.pallas import tpu as pltpu

F32 = jnp.float32
BF16 = jnp.bfloat16

LN_EPS = 1e-5
SUBLANES = 8
LANES = 128
MXU_TILE = 256
VMEM_LIMIT_BYTES = 56 * 1024 * 1024

SSD_HEAD_DIM = 64
SSD_GROUPS = 2
SSD_STATE = 128
SSD_CONV = 4
SSD_CHUNK = 128
N_MOD = 9

NB = SUBLANES
FFN_TT = 64
S5_SUB_T = 32


def _dot(a, b):
    return jnp.dot(a, b, preferred_element_type=F32)


def _sigmoid(x):
    return 1.0 / (1.0 + jnp.exp(-x))


def _silu(x):
    return x * _sigmoid(x)


def _softplus(x):
    return jnp.maximum(x, 0.0) + jnp.log1p(jnp.exp(-jnp.abs(x)))


def _layer_norm(y, g, b):
    mu = jnp.mean(y, axis=-1, keepdims=True)
    yc = y - mu
    var = jnp.mean(yc * yc, axis=-1, keepdims=True)
    return yc * lax.rsqrt(var + LN_EPS) * g + b


def _resident(shape):
    nd = len(shape)
    return pl.BlockSpec(shape, lambda *_: (0,) * nd, pipeline_mode=pl.Buffered(1))


def _ada_kernel(c_ref, w_ref, b_ref, o_ref):
    cs = _silu(c_ref[...]).astype(BF16)
    o_ref[...] = _dot(cs, w_ref[...].astype(BF16)) + b_ref[...]


def _ada_call(c, w_ada, b_ada):
    bsz, d = c.shape
    n = w_ada.shape[1]
    bn = n // 8
    return pl.pallas_call(
        _ada_kernel,
        grid=(n // bn,),
        in_specs=[pl.BlockSpec((bsz, d), lambda j: (0, 0)),
                  pl.BlockSpec((d, bn), lambda j: (0, j)),
                  pl.BlockSpec((1, bn), lambda j: (0, j))],
        out_specs=pl.BlockSpec((bsz, bn), lambda j: (0, j)),
        out_shape=jax.ShapeDtypeStruct((bsz, n), F32),
        compiler_params=pltpu.CompilerParams(dimension_semantics=("arbitrary",)),
        name="ada_mod",
    )(c, w_ada, b_ada.reshape(1, n))


def _s5_prep_kernel(ar_ref, ai_ref, ldt_ref, br_ref, bi_ref,
                    abr_ref, abi_ref, bbr_ref, bbi_ref):
    ar, ai = ar_ref[...], ai_ref[...]
    dt = jnp.exp(ldt_ref[...])
    mag = jnp.exp(dt * ar)
    ab_re = mag * jnp.cos(dt * ai)
    ab_im = mag * jnp.sin(dt * ai)
    den = ar * ar + ai * ai
    nr, ni = ab_re - 1.0, ab_im
    f_re = (nr * ar + ni * ai) / den
    f_im = (ni * ar - nr * ai) / den
    br, bi = br_ref[...], bi_ref[...]
    abr_ref[...] = ab_re
    abi_ref[...] = ab_im
    bbr_ref[...] = f_re * br - f_im * bi
    bbi_ref[...] = f_re * bi + f_im * br


def _s5_prep_call(a_re, a_im, log_dt, b_re, b_im):
    g, p = a_re.shape
    h = b_re.shape[-1]
    n = g * p
    row = lambda v: v.reshape(1, n)
    to_hn = lambda v: v.reshape(n, h).T
    vec = jax.ShapeDtypeStruct((1, n), F32)
    mat = jax.ShapeDtypeStruct((h, n), F32)
    return pl.pallas_call(
        _s5_prep_kernel,
        out_shape=(vec, vec, mat, mat),
        name="s5_prep",
    )(row(a_re), row(a_im), row(jnp.repeat(log_dt, p)), to_hn(b_re), to_hn(b_im))


def _swiglu_residual(x2, h2, gate2, w1_ref, w3_ref, w2_ref, alpha):
    hb = h2.astype(BF16)
    acc = None
    for lo, hi in _ffn_chunks(w1_ref.shape[1]):
        a = _dot(hb, w1_ref[:, lo:hi])
        b = _dot(hb, w3_ref[:, lo:hi])
        gg = (_silu(a) * b).astype(BF16)
        part = _dot(gg, w2_ref[lo:hi, :])
        acc = part if acc is None else acc + part
    return alpha * x2 + (0.5 * gate2) * acc


def _delayed_epilogue(step, n_tiles, compute, finish):
    @pl.when(step == 0)
    def _():
        compute()

    @pl.when(jnp.logical_and(step > 0, step < n_tiles))
    def _():
        finish()
        compute()

    @pl.when(step == n_tiles)
    def _():
        finish()


def _ffn1_kernel(x_ref, mod_ref, w1_ref, w3_ref, w2_ref, ln_ref, o_ref, pend_ref, tmp_ref,
                 *, alpha, n_tiles):
    nb, tt, d = x_ref.shape

    def compute():
        sh, sc, gt = mod_ref[0], mod_ref[1], mod_ref[2]
        x3 = x_ref[...]
        h3 = x3 * (1.0 + sc)[:, None, :] + sh[:, None, :]
        gate3 = jnp.broadcast_to(gt[:, None, :], (nb, tt, d))
        pend_ref[...] = _swiglu_residual(x3.reshape(nb * tt, d), h3.reshape(nb * tt, d),
                                         gate3.reshape(nb * tt, d), w1_ref, w3_ref, w2_ref, alpha)

    def finish():
        out = _layer_norm(pend_ref[...], ln_ref[0:1, :], ln_ref[1:2, :])
        for j in range(d // LANES):
            for b in range(nb):
                tmp_ref[j, pl.ds(b, tt, stride=nb), :] = out[b * tt:(b + 1) * tt,
                                                             j * LANES:(j + 1) * LANES]
            o_ref[:, j * LANES:(j + 1) * LANES] = tmp_ref[j]

    _delayed_epilogue(pl.program_id(0), n_tiles, compute, finish)


def _ffn2_kernel(x_ref, y_ref, mod2_ref, mod3_ref, wo_ref, ln_ref,
                 w1_ref, w3_ref, w2_ref, o_ref, pend_ref, tmp_ref,
                 *, alpha, n_tiles):
    nb, tt, d = o_ref.shape

    def compute():
        x1 = x_ref[...].reshape(tt, nb, d)
        m = _dot(y_ref[...], wo_ref[...]).reshape(tt, nb, d)
        x2 = _layer_norm((alpha * x1 + mod2_ref[2][None] * m).reshape(tt * nb, d),
                         ln_ref[2:3, :], ln_ref[3:4, :])
        sh, sc, gt = mod3_ref[0], mod3_ref[1], mod3_ref[2]
        h3 = x2.reshape(tt, nb, d) * (1.0 + sc)[None] + sh[None]
        gate3 = jnp.broadcast_to(gt[None], (tt, nb, d))
        pend_ref[...] = _swiglu_residual(x2, h3.reshape(tt * nb, d), gate3.reshape(tt * nb, d),
                                         w1_ref, w3_ref, w2_ref, alpha)

    def finish():
        out = _layer_norm(pend_ref[...], ln_ref[4:5, :], ln_ref[5:6, :])
        for j in range(d // LANES):
            tmp_ref[j] = out[:, j * LANES:(j + 1) * LANES]
            for b in range(nb):
                o_ref[b, :, j * LANES:(j + 1) * LANES] = tmp_ref[j, pl.ds(b, tt, stride=nb), :]

    _delayed_epilogue(pl.program_id(0), n_tiles, compute, finish)


def _ffn_chunks(d_ff):
    assert d_ff % MXU_TILE == 0
    cut = (d_ff // MXU_TILE + 1) // 2 * MXU_TILE
    return ((0, cut), (cut, d_ff)) if cut < d_ff else ((0, d_ff),)


def _ffn_weights(w1, w3, w2):
    return w1.astype(BF16), w3.astype(BF16), w2.astype(BF16)


def _tile_maps(n_tiles, per_half):
    def split(t):
        return t // per_half, t % per_half
    in_tile = lambda s: split(jnp.minimum(s, n_tiles - 1))
    out_tile = lambda s: split(jnp.maximum(s - 1, 0))
    return in_tile, out_tile


def _ffn1_call(x, mod1, w1, w3, w2, ln_vecs, alpha):
    bsz, s, d = x.shape
    nh = bsz // NB
    tt = FFN_TT
    per_half = s // tt
    n_tiles = nh * per_half
    in_tile, out_tile = _tile_maps(n_tiles, per_half)
    w1c, w3c, w2c = _ffn_weights(w1, w3, w2)
    return pl.pallas_call(
        functools.partial(_ffn1_kernel, alpha=alpha, n_tiles=n_tiles),
        grid=(n_tiles + 1,),
        in_specs=[pl.BlockSpec((NB, tt, d), lambda s: (*in_tile(s), 0)),
                  pl.BlockSpec((None, 3, NB, d), lambda s: (in_tile(s)[0], 0, 0, 0)),
                  _resident(w1c.shape), _resident(w3c.shape), _resident(w2c.shape),
                  _resident(ln_vecs.shape)],
        out_specs=pl.BlockSpec((None, tt * NB, d), lambda s: (*out_tile(s), 0)),
        out_shape=jax.ShapeDtypeStruct((nh, s * NB, d), F32),
        scratch_shapes=[pltpu.VMEM((tt * NB, d), F32),
                        pltpu.VMEM((d // LANES, tt * NB, LANES), F32)],
        compiler_params=pltpu.CompilerParams(
            dimension_semantics=("arbitrary",), vmem_limit_bytes=VMEM_LIMIT_BYTES),
        name="ffn1_ln1",
    )(x, mod1, w1c, w3c, w2c, ln_vecs)


def _ffn2_call(x1t, ycat, mod2, mod3, w_out, ln_vecs, w1, w3, w2, alpha):
    nh, rows, d = x1t.shape
    s = rows // NB
    tt = FFN_TT
    per_half = s // tt
    n_tiles = nh * per_half
    in_tile, out_tile = _tile_maps(n_tiles, per_half)
    w1c, w3c, w2c = _ffn_weights(w1, w3, w2)
    tile_in = pl.BlockSpec((None, tt * NB, d), lambda s: (*in_tile(s), 0))
    mod_in = pl.BlockSpec((None, 3, NB, d), lambda s: (in_tile(s)[0], 0, 0, 0))
    return pl.pallas_call(
        functools.partial(_ffn2_kernel, alpha=alpha, n_tiles=n_tiles),
        grid=(n_tiles + 1,),
        in_specs=[tile_in, tile_in, mod_in, mod_in,
                  _resident(w_out.shape), _resident(ln_vecs.shape),
                  _resident(w1c.shape), _resident(w3c.shape), _resident(w2c.shape)],
        out_specs=pl.BlockSpec((NB, tt, d), lambda s: (*out_tile(s), 0)),
        out_shape=jax.ShapeDtypeStruct((nh * NB, s, d), F32),
        scratch_shapes=[pltpu.VMEM((tt * NB, d), F32),
                        pltpu.VMEM((d // LANES, tt * NB, LANES), F32)],
        compiler_params=pltpu.CompilerParams(
            dimension_semantics=("arbitrary",), vmem_limit_bytes=VMEM_LIMIT_BYTES),
        name="outproj_ln2_ffn2_ln3",
    )(x1t, ycat, mod2, mod3, w_out.astype(BF16), ln_vecs, w1c, w3c, w2c)


def _mixer_kernel(x_ref, mod_ref, win_ref, cw_ref, vec_ref, hexp3_ref,
                  abr_ref, abi_ref, bmat_ref, cmat_ref, wg_ref,
                  o_ref,
                  z_buf, xbc_buf, act_buf, u_buf, acs_buf, y_buf, cdec_buf,
                  hstate, bu0_buf, bu1_buf, xb0_buf, xb1_buf, y5_buf, s5_state,
                  *, n_heads, n_groups):
    rows, d = x_ref.shape
    nb = NB
    tc = rows // nb
    halo = (SSD_CONV - 1) * nb
    ssd_w = z_buf.shape[1]
    conv_ch = xbc_buf.shape[0] * LANES
    xp = ssd_w // LANES
    assert SSD_STATE == LANES
    hpg = n_heads // n_groups
    gw = hpg * SSD_HEAD_DIM
    n_half, hk = bmat_ref.shape[0], bmat_ref.shape[1]
    sk = bmat_ref.shape[2] // 2
    s5_w = n_half * hk
    c_idx = pl.program_id(1)
    o_x, o_u, o_dt = ssd_w, ssd_w + conv_ch, ssd_w + conv_ch + s5_w
    cb_row = vec_ref[0:1, :]
    nw_row, s5d_row = vec_ref[1:2, 0:ssd_w], vec_ref[1:2, ssd_w:ssd_w + s5_w]
    dfull_row, bg_row = vec_ref[2:3, 0:ssd_w], vec_ref[2:3, ssd_w:ssd_w + s5_w]
    dtb_row, alog_row = vec_ref[3:4, 0:LANES], vec_ref[3:4, LANES:2 * LANES]

    @pl.when(c_idx == 0)
    def _():
        xbc_buf[:, 0:halo, :] = jnp.zeros((conv_ch // LANES, halo, LANES), F32)
        hstate[...] = jnp.zeros(hstate.shape, F32)
        s5_state[...] = jnp.zeros(s5_state.shape, F32)

    sh, sc = mod_ref[0], mod_ref[1]
    hb = (x_ref[...].reshape(tc, nb, d) * (1.0 + sc)[None] + sh[None]).reshape(rows, d).astype(BF16)
    xbc = _dot(hb, win_ref[:, o_x:o_u])
    for j in range(conv_ch // LANES):
        xbc_buf[j, halo:halo + rows, :] = xbc[:, j * LANES:(j + 1) * LANES]
    dt_raw = _dot(hb, win_ref[:, o_dt:o_dt + LANES])
    for k in range(n_half):
        u_buf[k] = _dot(hb, win_ref[:, o_u + k * hk:o_u + (k + 1) * hk])
    z_buf[...] = _dot(hb, win_ref[:, 0:o_x])

    for j in range(conv_ch // LANES):
        cs = slice(j * LANES, (j + 1) * LANES)
        conv = cb_row[:, cs]
        for k in range(SSD_CONV):
            conv = conv + cw_ref[k:k + 1, cs] * xbc_buf[j, k * nb:k * nb + rows, :]
        act_buf[j] = _silu(conv)
        xbc_buf[j, 0:halo, :] = xbc_buf[j, rows:rows + halo, :]

    dt = _softplus(dt_raw + dtb_row)
    acs = dt * (-jnp.exp(alog_row))
    shift = nb
    while shift < rows:
        acs = acs + jnp.concatenate([jnp.zeros((shift, LANES), F32), acs[:rows - shift]], axis=0)
        shift *= 2
    acs_buf[...] = acs
    hi = acs.astype(BF16)
    r1 = acs - hi.astype(F32)
    mid = r1.astype(BF16)
    lo = (r1 - mid.astype(F32)).astype(BF16)
    acs_full = _dot(jnp.concatenate([hi, mid, lo], axis=-1), hexp3_ref[...])
    dt_full = _dot(dt.astype(BF16), hexp3_ref[0:LANES, :])
    last = acs_full[rows - nb:rows, :]
    cdec_buf[...] = jnp.exp(last)
    dec = jnp.exp(last[None] - acs_full.reshape(tc, nb, ssd_w)).reshape(rows, ssd_w)
    eac = jnp.exp(acs_full)
    for j in range(xp):
        cs = slice(j * LANES, (j + 1) * LANES)
        xs = act_buf[j]
        xin = xs * dt_full[:, cs]
        y_buf[j] = dfull_row[:, cs] * xs
        act_buf[j] = xin
        xbc_buf[j, halo:halo + rows, :] = xin * dec[:, cs]
        xbc_buf[xp + j, halo:halo + rows, :] = eac[:, cs]

    causal = (lax.broadcasted_iota(jnp.int32, (tc, tc), 0)
              >= lax.broadcasted_iota(jnp.int32, (tc, tc), 1))
    gp = gw // LANES
    lane_head = lax.broadcasted_iota(jnp.int32, (tc, gw), 1) // SSD_HEAD_DIM

    def ssd_group(b, g, rsel, acs_b, acs_t, cdec):
        gather = lambda buf, p0, rs: jnp.concatenate(
            [buf[p0 + g * gp + jj, rs, :] for jj in range(gp)], axis=-1)
        rsel_h = pl.ds(halo + b, tc, stride=nb)
        xin_g = gather(act_buf, 0, rsel)
        xdec_g, eacs_g = gather(xbc_buf, 0, rsel_h), gather(xbc_buf, xp, rsel_h)
        bm = act_buf[xp + g, rsel, :]
        cm_b = act_buf[xp + n_groups + g, rsel, :].astype(BF16)
        cbm = lax.dot_general(cm_b, bm.astype(BF16), (((1,), (1,)), ((), ())),
                              preferred_element_type=F32)
        h_prev = hstate[b, g]
        y_off = _dot(cm_b, h_prev.astype(BF16)) * eacs_g
        mms, xblk = [], []
        for z in range(hpg):
            hd = g * hpg + z
            seg = acs_b[:, hd:hd + 1] - acs_t[hd:hd + 1, :]
            lm = jnp.where(causal, jnp.exp(seg), 0.0)
            mms.append((cbm * lm).astype(BF16))
            xblk.append(jnp.where(lane_head == z, xin_g, 0.0).astype(BF16))
        y_g = _dot(jnp.concatenate(mms, axis=-1), jnp.concatenate(xblk, axis=0)) + y_off
        for jj in range(gp):
            y_buf[g * gp + jj, rsel, :] = (y_buf[g * gp + jj, rsel, :]
                                           + y_g[:, jj * LANES:(jj + 1) * LANES])
        st = _dot(bm.T.astype(BF16), xdec_g.astype(BF16))
        hstate[b, g] = cdec[:, g * gw:(g + 1) * gw] * h_prev + st

    def ssd_one(b):
        rsel = pl.ds(b, tc, stride=nb)
        acs_b = acs_buf[rsel, :]
        acs_t = acs_b.T
        cdec = cdec_buf[pl.ds(b, 1), :]
        for g in range(n_groups):
            ssd_group(b, g, rsel, acs_b, acs_t, cdec)

    sub_rows = S5_SUB_T * nb
    n_units = (tc // S5_SUB_T) * n_half
    assert n_units == nb and n_half == 2
    bu_bufs, xb_bufs = (bu0_buf, bu1_buf), (xb0_buf, xb1_buf)

    def unit_rows(i):
        start = (i // n_half) * sub_rows
        return pl.ds(start if isinstance(i, int) else pl.multiple_of(start, sub_rows), sub_rows)

    def s5_in(i, k):
        bu_bufs[k][...] = _dot(u_buf[k, unit_rows(i), :].astype(BF16), bmat_ref[k])

    def s5_scan(k):
        bu, xb = bu_bufs[k], xb_bufs[k]
        a_r, a_i = abr_ref[k], abi_ref[k]
        xr, xi = s5_state[k, 0], s5_state[k, 1]
        for t in range(0, S5_SUB_T, 2):
            res_r, res_i = [], []
            for tt in (t, t + 1):
                rr = slice(tt * nb, (tt + 1) * nb)
                xr, xi = (a_r * xr - a_i * xi + bu[rr, 0:sk],
                          a_r * xi + a_i * xr + bu[rr, sk:2 * sk])
                res_r.append(xr)
                res_i.append(xi)
            r2 = slice(t * nb, (t + 2) * nb)
            xb[r2, 0:sk] = jnp.concatenate(res_r, axis=0).astype(BF16)
            xb[r2, sk:2 * sk] = jnp.concatenate(res_i, axis=0).astype(BF16)
        s5_state[k, 0] = xr
        s5_state[k, 1] = xi

    def s5_out(i, k):
        y5_buf[k, unit_rows(i), :] = _dot(xb_bufs[k][...], cmat_ref[k])

    def stage(i, k, first=False, last=False):
        s5_scan(k)
        if not last:
            s5_in(i + 1, 1 - k)
        if not first:
            s5_out(i - 1, 1 - k)
        ssd_one(i)

    s5_in(0, 0)
    stage(0, 0, first=True)

    def loop_body(p, carry):
        stage(2 * p + 1, 1)
        stage(2 * p + 2, 0)
        return carry

    lax.fori_loop(0, (n_units - 2) // 2, loop_body, 0)
    stage(n_units - 1, 1, last=True)
    s5_out(n_units - 1, 1)

    y = jnp.concatenate([y_buf[j] for j in range(xp)], axis=-1)
    y = y * _silu(z_buf[...])
    nw = ssd_w // n_groups
    parts = []
    for g in range(n_groups):
        yg = y[:, g * nw:(g + 1) * nw]
        parts.append(yg * lax.rsqrt(jnp.mean(yg * yg, axis=-1, keepdims=True) + LN_EPS))
    o_ref[:, 0:ssd_w] = (jnp.concatenate(parts, axis=-1) * nw_row).astype(o_ref.dtype)

    y5 = jnp.concatenate([y5_buf[k] + u_buf[k] * s5d_row[:, k * hk:(k + 1) * hk]
                          for k in range(n_half)], axis=-1)
    gl = jax.nn.gelu(y5, approximate=True)
    gate = _sigmoid(_dot(gl.astype(BF16), wg_ref[...]) + bg_row)
    o_ref[:, ssd_w:ssd_w + s5_w] = (gl * gate).astype(o_ref.dtype)


def _mixer_call(x1t, mod2, w_in, conv_w, conv_b, dt_bias, a_log, d_ssd, ssd_norm_w,
                ab_re, ab_im, bmat, cmat, s5_d, w_glu, b_glu):
    nh, rows_total, d = x1t.shape
    s = rows_total // NB
    ssd_w = ssd_norm_w.shape[0]
    n_heads = d_ssd.shape[0]
    conv_ch = conv_w.shape[1]
    s5_w = s5_d.shape[0]
    n_groups = SSD_GROUPS
    tc = SSD_CHUNK
    rows = tc * NB
    halo = (SSD_CONV - 1) * NB

    assert conv_ch == d and ssd_w + s5_w == d
    o1, o2, o3 = ssd_w, ssd_w + conv_ch, ssd_w + conv_ch + n_heads
    w_all = jnp.concatenate([w_in[:, :o2], w_in[:, o3:], w_in[:, o2:o3],
                             jnp.zeros((d, LANES - n_heads), w_in.dtype)], axis=1).astype(BF16)
    lane_pad = lambda v: jnp.pad(v, (0, LANES - n_heads))
    vecs = jnp.stack([conv_b,
                      jnp.concatenate([ssd_norm_w, s5_d]),
                      jnp.concatenate([jnp.repeat(d_ssd, SSD_HEAD_DIM), b_glu]),
                      jnp.concatenate([lane_pad(dt_bias), lane_pad(a_log),
                                       jnp.zeros((d - 2 * LANES,), F32)])])
    hexp = (jnp.arange(LANES)[:, None] == (jnp.arange(ssd_w) // SSD_HEAD_DIM)[None, :]).astype(BF16)
    hexp = jnp.concatenate([hexp] * 3, axis=0)

    n_half, hk, sk2 = bmat.shape
    kern = functools.partial(_mixer_kernel, n_heads=n_heads, n_groups=n_groups)
    xmap = lambda h, c: (h, c, 0)
    return pl.pallas_call(
        kern,
        grid=(nh, s // tc),
        in_specs=[pl.BlockSpec((None, rows, d), xmap),
                  pl.BlockSpec((None, 3, NB, d), lambda h, c: (h, 0, 0, 0)),
                  _resident(w_all.shape), _resident(conv_w.shape), _resident(vecs.shape),
                  _resident(hexp.shape),
                  _resident(ab_re.shape), _resident(ab_im.shape),
                  _resident(bmat.shape), _resident(cmat.shape), _resident(w_glu.shape)],
        out_specs=pl.BlockSpec((None, rows, ssd_w + s5_w), xmap),
        out_shape=jax.ShapeDtypeStruct((nh, rows_total, ssd_w + s5_w), BF16),
        scratch_shapes=[
            pltpu.VMEM((rows, ssd_w), F32),
            pltpu.VMEM((conv_ch // LANES, rows + halo, LANES), F32),
            pltpu.VMEM((conv_ch // LANES, rows, LANES), F32),
            pltpu.VMEM((n_half, rows, hk), F32),
            pltpu.VMEM((rows, LANES), F32),
            pltpu.VMEM((ssd_w // LANES, rows, LANES), F32),
            pltpu.VMEM((NB, ssd_w), F32),
            pltpu.VMEM((NB, n_groups, SSD_STATE, ssd_w // n_groups), F32),
            pltpu.VMEM((S5_SUB_T * NB, sk2), F32),
            pltpu.VMEM((S5_SUB_T * NB, sk2), F32),
            pltpu.VMEM((S5_SUB_T * NB, sk2), BF16),
            pltpu.VMEM((S5_SUB_T * NB, sk2), BF16),
            pltpu.VMEM((n_half, rows, hk), F32),
            pltpu.VMEM((n_half, 2, NB, sk2 // 2), F32),
        ],
        compiler_params=pltpu.CompilerParams(
            dimension_semantics=("arbitrary", "arbitrary"), vmem_limit_bytes=VMEM_LIMIT_BYTES),
        name="mixer",
    )(x1t, mod2, w_all, conv_w, vecs, hexp, ab_re, ab_im, bmat, cmat, w_glu.astype(BF16))


def _s5_matrices(ab_re, ab_im, bb_re, bb_im, c_re, c_im):
    h, n = bb_re.shape
    g, _, p = c_re.shape
    n_half = 2
    gh = g // n_half
    eye = jnp.eye(gh, dtype=F32)

    def b_block(bb):
        v = bb.reshape(h, n_half, gh, p)
        return jnp.einsum("hkgp,gG->kghGp", v, eye).reshape(n_half, gh * h, gh * p)

    def c_block(c):
        v = c.reshape(n_half, gh, h, p)
        return jnp.einsum("kghp,gG->kgpGh", v, eye).reshape(n_half, gh * p, gh * h)

    bmat = jnp.concatenate([b_block(bb_re), b_block(bb_im)], axis=-1).astype(BF16)
    cmat = jnp.concatenate([c_block(c_re), c_block(-c_im)], axis=1).astype(BF16)
    bc = lambda v: jnp.broadcast_to(v.reshape(n_half, 1, n // n_half), (n_half, NB, n // n_half))
    return bc(ab_re), bc(ab_im), bmat, cmat


def kernel(x, c, w_ada, b_ada, ffn1_w1, ffn1_w3, ffn1_w2, ln1_g, ln1_b, w_in, conv_w, conv_b,
           dt_bias, a_log, d_ssd, ssd_norm_w, s5_a_re, s5_a_im, s5_log_dt, s5_b_re, s5_b_im,
           s5_c_re, s5_c_im, s5_d, w_glu, b_glu, w_out, ln2_g, ln2_b, ffn2_w1, ffn2_w3, ffn2_w2,
           ln3_g, ln3_b):
    bsz, s, d = x.shape
    depth = w_ada.shape[0]
    alpha = (2 * depth) ** 0.25
    assert depth == 1
    assert bsz % NB == 0 and s % SSD_CHUNK == 0 and s % FFN_TT == 0
    nh = bsz // NB
    for l in range(depth):
        mod = _ada_call(c, w_ada[l], b_ada[l]).reshape(bsz, N_MOD // 3, 3, d)
        mod = mod.reshape(nh, NB, N_MOD // 3, 3, d).transpose(2, 0, 3, 1, 4)
        ab_re, ab_im, bb_re, bb_im = _s5_prep_call(s5_a_re[l], s5_a_im[l], s5_log_dt[l],
                                                   s5_b_re[l], s5_b_im[l])
        ab_re, ab_im, bmat, cmat = _s5_matrices(ab_re, ab_im, bb_re, bb_im, s5_c_re[l], s5_c_im[l])
        ln_vecs = jnp.stack([ln1_g[l], ln1_b[l], ln2_g[l], ln2_b[l], ln3_g[l], ln3_b[l]])
        x1t = _ffn1_call(x, mod[0], ffn1_w1[l], ffn1_w3[l], ffn1_w2[l], ln_vecs, alpha)
        ycat = _mixer_call(x1t, mod[1], w_in[l], conv_w[l], conv_b[l], dt_bias[l], a_log[l],
                           d_ssd[l], ssd_norm_w[l], ab_re, ab_im, bmat, cmat, s5_d[l],
                           w_glu[l], b_glu[l])
        x = _ffn2_call(x1t, ycat, mod[1], mod[2], w_out[l], ln_vecs,
                       ffn2_w1[l], ffn2_w3[l], ffn2_w2[l], alpha)
    return x
```

```python
import functools

import jax
import jax.numpy as jnp
from jax import lax
from jax.experimental import pallas as pl
from jax.experimental.pallas import tpu as pltpu

F32 = jnp.float32
BF16 = jnp.bfloat16

LN_EPS = 1e-5
SUBLANES = 8
LANES = 128
MXU_TILE = 256
VMEM_LIMIT_BYTES = 56 * 1024 * 1024

SSD_HEAD_DIM = 64
SSD_GROUPS = 2
SSD_STATE = 128
SSD_CONV = 4
SSD_CHUNK = 128
N_MOD = 9

NB = SUBLANES
FFN_TT = 64
S5_SUB_T = 32


def _dot(a, b):
    return jnp.dot(a, b, preferred_element_type=F32)


def _sigmoid(x):
    return 1.0 / (1.0 + jnp.exp(-x))


def _silu(x):
    return x * _sigmoid(x)


def _softplus(x):
    return jnp.maximum(x, 0.0) + jnp.log1p(jnp.exp(-jnp.abs(x)))


def _layer_norm(y, g, b):
    mu = jnp.mean(y, axis=-1, keepdims=True)
    yc = y - mu
    var = jnp.mean(yc * yc, axis=-1, keepdims=True)
    return yc * lax.rsqrt(var + LN_EPS) * g + b


def _resident(shape):
    nd = len(shape)
    return pl.BlockSpec(shape, lambda *_: (0,) * nd, pipeline_mode=pl.Buffered(1))


def _ada_kernel(c_ref, w_ref, b_ref, o_ref):
    cs = _silu(c_ref[...]).astype(BF16)
    o_ref[...] = _dot(cs, w_ref[...].astype(BF16)) + b_ref[...]


def _ada_call(c, w_ada, b_ada):
    bsz, d = c.shape
    n = w_ada.shape[1]
    bn = n // 8
    return pl.pallas_call(
        _ada_kernel,
        grid=(n // bn,),
        in_specs=[pl.BlockSpec((bsz, d), lambda j: (0, 0)),
                  pl.BlockSpec((d, bn), lambda j: (0, j)),
                  pl.BlockSpec((1, bn), lambda j: (0, j))],
        out_specs=pl.BlockSpec((bsz, bn), lambda j: (0, j)),
        out_shape=jax.ShapeDtypeStruct((bsz, n), F32),
        compiler_params=pltpu.CompilerParams(dimension_semantics=("arbitrary",)),
        name="ada_mod",
    )(c, w_ada, b_ada.reshape(1, n))


def _s5_prep_kernel(ar_ref, ai_ref, ldt_ref, br_ref, bi_ref,
                    abr_ref, abi_ref, bbr_ref, bbi_ref):
    ar, ai = ar_ref[...], ai_ref[...]
    dt = jnp.exp(ldt_ref[...])
    mag = jnp.exp(dt * ar)
    ab_re = mag * jnp.cos(dt * ai)
    ab_im = mag * jnp.sin(dt * ai)
    den = ar * ar + ai * ai
    nr, ni = ab_re - 1.0, ab_im
    f_re = (nr * ar + ni * ai) / den
    f_im = (ni * ar - nr * ai) / den
    br, bi = br_ref[...], bi_ref[...]
    abr_ref[...] = ab_re
    abi_ref[...] = ab_im
    bbr_ref[...] = f_re * br - f_im * bi
    bbi_ref[...] = f_re * bi + f_im * br


def _s5_prep_call(a_re, a_im, log_dt, b_re, b_im):
    g, p = a_re.shape
    h = b_re.shape[-1]
    n = g * p
    row = lambda v: v.reshape(1, n)
    to_hn = lambda v: v.reshape(n, h).T
    vec = jax.ShapeDtypeStruct((1, n), F32)
    mat = jax.ShapeDtypeStruct((h, n), F32)
    return pl.pallas_call(
        _s5_prep_kernel,
        out_shape=(vec, vec, mat, mat),
        name="s5_prep",
    )(row(a_re), row(a_im), row(jnp.repeat(log_dt, p)), to_hn(b_re), to_hn(b_im))


def _swiglu_residual(x2, h2, gate2, w1_ref, w3_ref, w2_ref, alpha):
    hb = h2.astype(BF16)
    acc = None
    for lo, hi in _ffn_chunks(w1_ref.shape[1]):
        a = _dot(hb, w1_ref[:, lo:hi])
        b = _dot(hb, w3_ref[:, lo:hi])
        gg = (_silu(a) * b).astype(BF16)
        part = _dot(gg, w2_ref[lo:hi, :])
        acc = part if acc is None else acc + part
    return alpha * x2 + (0.5 * gate2) * acc


def _delayed_epilogue(step, n_tiles, compute, finish):
    @pl.when(step == 0)
    def _():
        compute()

    @pl.when(jnp.logical_and(step > 0, step < n_tiles))
    def _():
        finish()
        compute()

    @pl.when(step == n_tiles)
    def _():
        finish()


def _ffn1_kernel(x_ref, mod_ref, w1_ref, w3_ref, w2_ref, ln_ref, o_ref, pend_ref, tmp_ref,
                 *, alpha, n_tiles):
    nb, tt, d = x_ref.shape

    def compute():
        sh, sc, gt = mod_ref[0], mod_ref[1], mod_ref[2]
        x3 = x_ref[...]
        h3 = x3 * (1.0 + sc)[:, None, :] + sh[:, None, :]
        gate3 = jnp.broadcast_to(gt[:, None, :], (nb, tt, d))
        pend_ref[...] = _swiglu_residual(x3.reshape(nb * tt, d), h3.reshape(nb * tt, d),
                                         gate3.reshape(nb * tt, d), w1_ref, w3_ref, w2_ref, alpha)

    def finish():
        out = _layer_norm(pend_ref[...], ln_ref[0:1, :], ln_ref[1:2, :])
        for j in range(d // LANES):
            for b in range(nb):
                tmp_ref[j, pl.ds(b, tt, stride=nb), :] = out[b * tt:(b + 1) * tt,
                                                             j * LANES:(j + 1) * LANES]
            o_ref[:, j * LANES:(j + 1) * LANES] = tmp_ref[j]

    _delayed_epilogue(pl.program_id(0), n_tiles, compute, finish)


def _ffn2_kernel(x_ref, y_ref, mod2_ref, mod3_ref, wo_ref, ln_ref,
                 w1_ref, w3_ref, w2_ref, o_ref, pend_ref, tmp_ref,
                 *, alpha, n_tiles):
    nb, tt, d = o_ref.shape

    def compute():
        x1 = x_ref[...].reshape(tt, nb, d)
        m = _dot(y_ref[...], wo_ref[...]).reshape(tt, nb, d)
        x2 = _layer_norm((alpha * x1 + mod2_ref[2][None] * m).reshape(tt * nb, d),
                         ln_ref[2:3, :], ln_ref[3:4, :])
        sh, sc, gt = mod3_ref[0], mod3_ref[1], mod3_ref[2]
        h3 = x2.reshape(tt, nb, d) * (1.0 + sc)[None] + sh[None]
        gate3 = jnp.broadcast_to(gt[None], (tt, nb, d))
        pend_ref[...] = _swiglu_residual(x2, h3.reshape(tt * nb, d), gate3.reshape(tt * nb, d),
                                         w1_ref, w3_ref, w2_ref, alpha)

    def finish():
        out = _layer_norm(pend_ref[...], ln_ref[4:5, :], ln_ref[5:6, :])
        for j in range(d // LANES):
            tmp_ref[j] = out[:, j * LANES:(j + 1) * LANES]
            for b in range(nb):
                o_ref[b, :, j * LANES:(j + 1) * LANES] = tmp_ref[j, pl.ds(b, tt, stride=nb), :]

    _delayed_epilogue(pl.program_id(0), n_tiles, compute, finish)


def _ffn_chunks(d_ff):
    assert d_ff % MXU_TILE == 0
    cut = (d_ff // MXU_TILE + 1) // 2 * MXU_TILE
    return ((0, cut), (cut, d_ff)) if cut < d_ff else ((0, d_ff),)


def _ffn_weights(w1, w3, w2):
    return w1.astype(BF16), w3.astype(BF16), w2.astype(BF16)


def _tile_maps(n_tiles, per_half):
    def split(t):
        return t // per_half, t % per_half
    in_tile = lambda s: split(jnp.minimum(s, n_tiles - 1))
    out_tile = lambda s: split(jnp.maximum(s - 1, 0))
    return in_tile, out_tile


def _ffn1_call(x, mod1, w1, w3, w2, ln_vecs, alpha):
    bsz, s, d = x.shape
    nh = bsz // NB
    tt = FFN_TT
    per_half = s // tt
    n_tiles = nh * per_half
    in_tile, out_tile = _tile_maps(n_tiles, per_half)
    w1c, w3c, w2c = _ffn_weights(w1, w3, w2)
    return pl.pallas_call(
        functools.partial(_ffn1_kernel, alpha=alpha, n_tiles=n_tiles),
        grid=(n_tiles + 1,),
        in_specs=[pl.BlockSpec((NB, tt, d), lambda s: (*in_tile(s), 0)),
                  pl.BlockSpec((None, 3, NB, d), lambda s: (in_tile(s)[0], 0, 0, 0)),
                  _resident(w1c.shape), _resident(w3c.shape), _resident(w2c.shape),
                  _resident(ln_vecs.shape)],
        out_specs=pl.BlockSpec((None, tt * NB, d), lambda s: (*out_tile(s), 0)),
        out_shape=jax.ShapeDtypeStruct((nh, s * NB, d), F32),
        scratch_shapes=[pltpu.VMEM((tt * NB, d), F32),
                        pltpu.VMEM((d // LANES, tt * NB, LANES), F32)],
        compiler_params=pltpu.CompilerParams(
            dimension_semantics=("arbitrary",), vmem_limit_bytes=VMEM_LIMIT_BYTES),
        name="ffn1_ln1",
    )(x, mod1, w1c, w3c, w2c, ln_vecs)


def _ffn2_call(x1t, ycat, mod2, mod3, w_out, ln_vecs, w1, w3, w2, alpha):
    nh, rows, d = x1t.shape
    s = rows // NB
    tt = FFN_TT
    per_half = s // tt
    n_tiles = nh * per_half
    in_tile, out_tile = _tile_maps(n_tiles, per_half)
    w1c, w3c, w2c = _ffn_weights(w1, w3, w2)
    tile_in = pl.BlockSpec((None, tt * NB, d), lambda s: (*in_tile(s), 0))
    mod_in = pl.BlockSpec((None, 3, NB, d), lambda s: (in_tile(s)[0], 0, 0, 0))
    return pl.pallas_call(
        functools.partial(_ffn2_kernel, alpha=alpha, n_tiles=n_tiles),
        grid=(n_tiles + 1,),
        in_specs=[tile_in, tile_in, mod_in, mod_in,
                  _resident(w_out.shape), _resident(ln_vecs.shape),
                  _resident(w1c.shape), _resident(w3c.shape), _resident(w2c.shape)],
        out_specs=pl.BlockSpec((NB, tt, d), lambda s: (*out_tile(s), 0)),
        out_shape=jax.ShapeDtypeStruct((nh * NB, s, d), F32),
        scratch_shapes=[pltpu.VMEM((tt * NB, d), F32),
                        pltpu.VMEM((d // LANES, tt * NB, LANES), F32)],
        compiler_params=pltpu.CompilerParams(
            dimension_semantics=("arbitrary",), vmem_limit_bytes=VMEM_LIMIT_BYTES),
        name="outproj_ln2_ffn2_ln3",
    )(x1t, ycat, mod2, mod3, w_out.astype(BF16), ln_vecs, w1c, w3c, w2c)


def _mixer_kernel(x_ref, mod_ref, win_ref, cw_ref, vec_ref, hexp3_ref,
                  abr_ref, abi_ref, bmat_ref, cmat_ref, wg_ref,
                  o_ref,
                  z_buf, xbc_buf, act_buf, u_buf, acs_buf, y_buf, cdec_buf,
                  hstate, bu0_buf, bu1_buf, xb0_buf, xb1_buf, y5_buf, s5_state,
                  *, n_heads, n_groups):
    rows, d = x_ref.shape
    nb = NB
    tc = rows // nb
    halo = (SSD_CONV - 1) * nb
    ssd_w = z_buf.shape[1]
    conv_ch = xbc_buf.shape[0] * LANES
    xp = ssd_w // LANES
    assert SSD_STATE == LANES
    hpg = n_heads // n_groups
    gw = hpg * SSD_HEAD_DIM
    n_half, hk = bmat_ref.shape[0], bmat_ref.shape[1]
    sk = bmat_ref.shape[2] // 2
    s5_w = n_half * hk
    c_idx = pl.program_id(1)
    o_x, o_u, o_dt = ssd_w, ssd_w + conv_ch, ssd_w + conv_ch + s5_w
    cb_row = vec_ref[0:1, :]
    nw_row, s5d_row = vec_ref[1:2, 0:ssd_w], vec_ref[1:2, ssd_w:ssd_w + s5_w]
    dfull_row, bg_row = vec_ref[2:3, 0:ssd_w], vec_ref[2:3, ssd_w:ssd_w + s5_w]
    dtb_row, alog_row = vec_ref[3:4, 0:LANES], vec_ref[3:4, LANES:2 * LANES]

    @pl.when(c_idx == 0)
    def _():
        xbc_buf[:, 0:halo, :] = jnp.zeros((conv_ch // LANES, halo, LANES), F32)
        hstate[...] = jnp.zeros(hstate.shape, F32)
        s5_state[...] = jnp.zeros(s5_state.shape, F32)

    sh, sc = mod_ref[0], mod_ref[1]
    hb = (x_ref[...].reshape(tc, nb, d) * (1.0 + sc)[None] + sh[None]).reshape(rows, d).astype(BF16)
    xbc = _dot(hb, win_ref[:, o_x:o_u])
    for j in range(conv_ch // LANES):
        xbc_buf[j, halo:halo + rows, :] = xbc[:, j * LANES:(j + 1) * LANES]
    dt_raw = _dot(hb, win_ref[:, o_dt:o_dt + LANES])
    for k in range(n_half):
        u_buf[k] = _dot(hb, win_ref[:, o_u + k * hk:o_u + (k + 1) * hk])
    z_buf[...] = _dot(hb, win_ref[:, 0:o_x])

    for j in range(conv_ch // LANES):
        cs = slice(j * LANES, (j + 1) * LANES)
        conv = cb_row[:, cs]
        for k in range(SSD_CONV):
            conv = conv + cw_ref[k:k + 1, cs] * xbc_buf[j, k * nb:k * nb + rows, :]
        act_buf[j] = _silu(conv)
        xbc_buf[j, 0:halo, :] = xbc_buf[j, rows:rows + halo, :]

    dt = _softplus(dt_raw + dtb_row)
    acs = dt * (-jnp.exp(alog_row))
    shift = nb
    while shift < rows:
        acs = acs + jnp.concatenate([jnp.zeros((shift, LANES), F32), acs[:rows - shift]], axis=0)
        shift *= 2
    acs_buf[...] = acs
    hi = acs.astype(BF16)
    r1 = acs - hi.astype(F32)
    mid = r1.astype(BF16)
    lo = (r1 - mid.astype(F32)).astype(BF16)
    acs_full = _dot(jnp.concatenate([hi, mid, lo], axis=-1), hexp3_ref[...])
    dt_full = _dot(dt.astype(BF16), hexp3_ref[0:LANES, :])
    last = acs_full[rows - nb:rows, :]
    cdec_buf[...] = jnp.exp(last)
    dec = jnp.exp(last[None] - acs_full.reshape(tc, nb, ssd_w)).reshape(rows, ssd_w)
    eac = jnp.exp(acs_full)
    for j in range(xp):
        cs = slice(j * LANES, (j + 1) * LANES)
        xs = act_buf[j]
        xin = xs * dt_full[:, cs]
        y_buf[j] = dfull_row[:, cs] * xs
        act_buf[j] = xin
        xbc_buf[j, halo:halo + rows, :] = xin * dec[:, cs]
        xbc_buf[xp + j, halo:halo + rows, :] = eac[:, cs]

    causal = (lax.broadcasted_iota(jnp.int32, (tc, tc), 0)
              >= lax.broadcasted_iota(jnp.int32, (tc, tc), 1))
    gp = gw // LANES
    lane_head = lax.broadcasted_iota(jnp.int32, (tc, gw), 1) // SSD_HEAD_DIM

    def ssd_group(b, g, rsel, acs_b, acs_t, cdec):
        gather = lambda buf, p0, rs: jnp.concatenate(
            [buf[p0 + g * gp + jj, rs, :] for jj in range(gp)], axis=-1)
        rsel_h = pl.ds(halo + b, tc, stride=nb)
        xin_g = gather(act_buf, 0, rsel)
        xdec_g, eacs_g = gather(xbc_buf, 0, rsel_h), gather(xbc_buf, xp, rsel_h)
        bm = act_buf[xp + g, rsel, :]
        cm_b = act_buf[xp + n_groups + g, rsel, :].astype(BF16)
        cbm = lax.dot_general(cm_b, bm.astype(BF16), (((1,), (1,)), ((), ())),
                              preferred_element_type=F32)
        h_prev = hstate[b, g]
        y_off = _dot(cm_b, h_prev.astype(BF16)) * eacs_g
        mms, xblk = [], []
        for z in range(hpg):
            hd = g * hpg + z
            seg = acs_b[:, hd:hd + 1] - acs_t[hd:hd + 1, :]
            lm = jnp.where(causal, jnp.exp(seg), 0.0)
            mms.append((cbm * lm).astype(BF16))
            xblk.append(jnp.where(lane_head == z, xin_g, 0.0).astype(BF16))
        y_g = _dot(jnp.concatenate(mms, axis=-1), jnp.concatenate(xblk, axis=0)) + y_off
        for jj in range(gp):
            y_buf[g * gp + jj, rsel, :] = (y_buf[g * gp + jj, rsel, :]
                                           + y_g[:, jj * LANES:(jj + 1) * LANES])
        st = _dot(bm.T.astype(BF16), xdec_g.astype(BF16))
        hstate[b, g] = cdec[:, g * gw:(g + 1) * gw] * h_prev + st

    def ssd_one(b):
        rsel = pl.ds(b, tc, stride=nb)
        acs_b = acs_buf[rsel, :]
        acs_t = acs_b.T
        cdec = cdec_buf[pl.ds(b, 1), :]
        for g in range(n_groups):
            ssd_group(b, g, rsel, acs_b, acs_t, cdec)

    sub_rows = S5_SUB_T * nb
    n_units = (tc // S5_SUB_T) * n_half
    assert n_units == nb and n_half == 2
    bu_bufs, xb_bufs = (bu0_buf, bu1_buf), (xb0_buf, xb1_buf)

    def unit_rows(i):
        start = (i // n_half) * sub_rows
        return pl.ds(start if isinstance(i, int) else pl.multiple_of(start, sub_rows), sub_rows)

    def s5_in(i, k):
        bu_bufs[k][...] = _dot(u_buf[k, unit_rows(i), :].astype(BF16), bmat_ref[k])

    def s5_scan(k):
        bu, xb = bu_bufs[k], xb_bufs[k]
        a_r, a_i = abr_ref[k], abi_ref[k]
        xr, xi = s5_state[k, 0], s5_state[k, 1]
        for t in range(0, S5_SUB_T, 2):
            res_r, res_i = [], []
            for tt in (t, t + 1):
                rr = slice(tt * nb, (tt + 1) * nb)
                xr, xi = (a_r * xr - a_i * xi + bu[rr, 0:sk],
                          a_r * xi + a_i * xr + bu[rr, sk:2 * sk])
                res_r.append(xr)
                res_i.append(xi)
            r2 = slice(t * nb, (t + 2) * nb)
            xb[r2, 0:sk] = jnp.concatenate(res_r, axis=0).astype(BF16)
            xb[r2, sk:2 * sk] = jnp.concatenate(res_i, axis=0).astype(BF16)
        s5_state[k, 0] = xr
        s5_state[k, 1] = xi

    def s5_out(i, k):
        y5_buf[k, unit_rows(i), :] = _dot(xb_bufs[k][...], cmat_ref[k])

    def stage(i, k, first=False, last=False):
        s5_scan(k)
        if not last:
            s5_in(i + 1, 1 - k)
        if not first:
            s5_out(i - 1, 1 - k)
        ssd_one(i)

    s5_in(0, 0)
    for i in range(n_units):
        stage(i, i % n_half, first=i == 0, last=i == n_units - 1)
    s5_out(n_units - 1, (n_units - 1) % n_half)

    y = jnp.concatenate([y_buf[j] for j in range(xp)], axis=-1)
    y = y * _silu(z_buf[...])
    nw = ssd_w // n_groups
    parts = []
    for g in range(n_groups):
        yg = y[:, g * nw:(g + 1) * nw]
        parts.append(yg * lax.rsqrt(jnp.mean(yg * yg, axis=-1, keepdims=True) + LN_EPS))
    o_ref[:, 0:ssd_w] = (jnp.concatenate(parts, axis=-1) * nw_row).astype(o_ref.dtype)

    y5 = jnp.concatenate([y5_buf[k] + u_buf[k] * s5d_row[:, k * hk:(k + 1) * hk]
                          for k in range(n_half)], axis=-1)
    gl = jax.nn.gelu(y5, approximate=True)
    gate = _sigmoid(_dot(gl.astype(BF16), wg_ref[...]) + bg_row)
    o_ref[:, ssd_w:ssd_w + s5_w] = (gl * gate).astype(o_ref.dtype)


def _mixer_call(x1t, mod2, w_in, conv_w, conv_b, dt_bias, a_log, d_ssd, ssd_norm_w,
                ab_re, ab_im, bmat, cmat, s5_d, w_glu, b_glu):
    nh, rows_total, d = x1t.shape
    s = rows_total // NB
    ssd_w = ssd_norm_w.shape[0]
    n_heads = d_ssd.shape[0]
    conv_ch = conv_w.shape[1]
    s5_w = s5_d.shape[0]
    n_groups = SSD_GROUPS
    tc = SSD_CHUNK
    rows = tc * NB
    halo = (SSD_CONV - 1) * NB

    assert conv_ch == d and ssd_w + s5_w == d
    o1, o2, o3 = ssd_w, ssd_w + conv_ch, ssd_w + conv_ch + n_heads
    w_all = jnp.concatenate([w_in[:, :o2], w_in[:, o3:], w_in[:, o2:o3],
                             jnp.zeros((d, LANES - n_heads), w_in.dtype)], axis=1).astype(BF16)
    lane_pad = lambda v: jnp.pad(v, (0, LANES - n_heads))
    vecs = jnp.stack([conv_b,
                      jnp.concatenate([ssd_norm_w, s5_d]),
                      jnp.concatenate([jnp.repeat(d_ssd, SSD_HEAD_DIM), b_glu]),
                      jnp.concatenate([lane_pad(dt_bias), lane_pad(a_log),
                                       jnp.zeros((d - 2 * LANES,), F32)])])
    hexp = (jnp.arange(LANES)[:, None] == (jnp.arange(ssd_w) // SSD_HEAD_DIM)[None, :]).astype(BF16)
    hexp = jnp.concatenate([hexp] * 3, axis=0)

    n_half, hk, sk2 = bmat.shape
    kern = functools.partial(_mixer_kernel, n_heads=n_heads, n_groups=n_groups)
    xmap = lambda h, c: (h, c, 0)
    return pl.pallas_call(
        kern,
        grid=(nh, s // tc),
        in_specs=[pl.BlockSpec((None, rows, d), xmap),
                  pl.BlockSpec((None, 3, NB, d), lambda h, c: (h, 0, 0, 0)),
                  _resident(w_all.shape), _resident(conv_w.shape), _resident(vecs.shape),
                  _resident(hexp.shape),
                  _resident(ab_re.shape), _resident(ab_im.shape),
                  _resident(bmat.shape), _resident(cmat.shape), _resident(w_glu.shape)],
        out_specs=pl.BlockSpec((None, rows, ssd_w + s5_w), xmap),
        out_shape=jax.ShapeDtypeStruct((nh, rows_total, ssd_w + s5_w), BF16),
        scratch_shapes=[
            pltpu.VMEM((rows, ssd_w), F32),
            pltpu.VMEM((conv_ch // LANES, rows + halo, LANES), F32),
            pltpu.VMEM((conv_ch // LANES, rows, LANES), F32),
            pltpu.VMEM((n_half, rows, hk), F32),
            pltpu.VMEM((rows, LANES), F32),
            pltpu.VMEM((ssd_w // LANES, rows, LANES), F32),
            pltpu.VMEM((NB, ssd_w), F32),
            pltpu.VMEM((NB, n_groups, SSD_STATE, ssd_w // n_groups), F32),
            pltpu.VMEM((S5_SUB_T * NB, sk2), F32),
            pltpu.VMEM((S5_SUB_T * NB, sk2), F32),
            pltpu.VMEM((S5_SUB_T * NB, sk2), BF16),
            pltpu.VMEM((S5_SUB_T * NB, sk2), BF16),
            pltpu.VMEM((n_half, rows, hk), F32),
            pltpu.VMEM((n_half, 2, NB, sk2 // 2), F32),
        ],
        compiler_params=pltpu.CompilerParams(
            dimension_semantics=("arbitrary", "arbitrary"), vmem_limit_bytes=VMEM_LIMIT_BYTES),
        name="mixer",
    )(x1t, mod2, w_all, conv_w, vecs, hexp, ab_re, ab_im, bmat, cmat, w_glu.astype(BF16))


def _s5_matrices(ab_re, ab_im, bb_re, bb_im, c_re, c_im):
    h, n = bb_re.shape
    g, _, p = c_re.shape
    n_half = 2
    gh = g // n_half
    eye = jnp.eye(gh, dtype=F32)

    def b_block(bb):
        v = bb.reshape(h, n_half, gh, p)
        return jnp.einsum("hkgp,gG->kghGp", v, eye).reshape(n_half, gh * h, gh * p)

    def c_block(c):
        v = c.reshape(n_half, gh, h, p)
        return jnp.einsum("kghp,gG->kgpGh", v, eye).reshape(n_half, gh * p, gh * h)

    bmat = jnp.concatenate([b_block(bb_re), b_block(bb_im)], axis=-1).astype(BF16)
    cmat = jnp.concatenate([c_block(c_re), c_block(-c_im)], axis=1).astype(BF16)
    bc = lambda v: jnp.broadcast_to(v.reshape(n_half, 1, n // n_half), (n_half, NB, n // n_half))
    return bc(ab_re), bc(ab_im), bmat, cmat


def kernel(x, c, w_ada, b_ada, ffn1_w1, ffn1_w3, ffn1_w2, ln1_g, ln1_b, w_in, conv_w, conv_b,
           dt_bias, a_log, d_ssd, ssd_norm_w, s5_a_re, s5_a_im, s5_log_dt, s5_b_re, s5_b_im,
           s5_c_re, s5_c_im, s5_d, w_glu, b_glu, w_out, ln2_g, ln2_b, ffn2_w1, ffn2_w3, ffn2_w2,
           ln3_g, ln3_b):
    bsz, s, d = x.shape
    depth = w_ada.shape[0]
    alpha = (2 * depth) ** 0.25
    assert depth == 1
    assert bsz % NB == 0 and s % SSD_CHUNK == 0 and s % FFN_TT == 0
    nh = bsz // NB
    for l in range(depth):
        mod = _ada_call(c, w_ada[l], b_ada[l]).reshape(bsz, N_MOD // 3, 3, d)
        mod = mod.reshape(nh, NB, N_MOD // 3, 3, d).transpose(2, 0, 3, 1, 4)
        ab_re, ab_im, bb_re, bb_im = _s5_prep_call(s5_a_re[l], s5_a_im[l], s5_log_dt[l],
                                                   s5_b_re[l], s5_b_im[l])
        ab_re, ab_im, bmat, cmat = _s5_matrices(ab_re, ab_im, bb_re, bb_im, s5_c_re[l], s5_c_im[l])
        ln_vecs = jnp.stack([ln1_g[l], ln1_b[l], ln2_g[l], ln2_b[l], ln3_g[l], ln3_b[l]])
        x1t = _ffn1_call(x, mod[0], ffn1_w1[l], ffn1_w3[l], ffn1_w2[l], ln_vecs, alpha)
        ycat = _mixer_call(x1t, mod[1], w_in[l], conv_w[l], conv_b[l], dt_bias[l], a_log[l],
                           d_ssd[l], ssd_norm_w[l], ab_re, ab_im, bmat, cmat, s5_d[l],
                           w_glu[l], b_glu[l])
        x = _ffn2_call(x1t, ycat, mod[1], mod[2], w_out[l], ln_vecs,
                       ffn2_w1[l], ffn2_w3[l], ffn2_w2[l], alpha)
    return x
```

```python
import functools

import jax
import jax.numpy as jnp
from jax import lax
from jax.experimental import pallas as pl
from jax.experimental.pallas import tpu as pltpu

F32 = jnp.float32
BF16 = jnp.bfloat16

LN_EPS = 1e-5
SUBLANES = 8
LANES = 128
MXU_TILE = 256
VMEM_LIMIT_BYTES = 56 * 1024 * 1024

SSD_HEAD_DIM = 64
SSD_GROUPS = 2
SSD_STATE = 128
SSD_CONV = 4
SSD_CHUNK = 128
N_MOD = 9

NB = SUBLANES
FFN_TT = 64
S5_SUB_T = 32


def _dot(a, b):
    return jnp.dot(a, b, preferred_element_type=F32)


def _sigmoid(x):
    return 1.0 / (1.0 + jnp.exp(-x))


def _silu(x):
    return x * _sigmoid(x)


def _softplus(x):
    return jnp.maximum(x, 0.0) + jnp.log1p(jnp.exp(-jnp.abs(x)))


def _layer_norm(y, g, b):
    mu = jnp.mean(y, axis=-1, keepdims=True)
    yc = y - mu
    var = jnp.mean(yc * yc, axis=-1, keepdims=True)
    return yc * lax.rsqrt(var + LN_EPS) * g + b


def _resident(shape):
    nd = len(shape)
    return pl.BlockSpec(shape, lambda *_: (0,) * nd, pipeline_mode=pl.Buffered(1))


def _ada_kernel(c_ref, w_ref, b_ref, o_ref):
    cs = _silu(c_ref[...]).astype(BF16)
    o_ref[...] = _dot(cs, w_ref[...].astype(BF16)) + b_ref[...]


def _ada_call(c, w_ada, b_ada):
    bsz, d = c.shape
    n = w_ada.shape[1]
    bn = n // 8
    return pl.pallas_call(
        _ada_kernel,
        grid=(n // bn,),
        in_specs=[pl.BlockSpec((bsz, d), lambda j: (0, 0)),
                  pl.BlockSpec((d, bn), lambda j: (0, j)),
                  pl.BlockSpec((1, bn), lambda j: (0, j))],
        out_specs=pl.BlockSpec((bsz, bn), lambda j: (0, j)),
        out_shape=jax.ShapeDtypeStruct((bsz, n), F32),
        compiler_params=pltpu.CompilerParams(dimension_semantics=("arbitrary",)),
        name="ada_mod",
    )(c, w_ada, b_ada.reshape(1, n))


def _s5_prep_kernel(ar_ref, ai_ref, ldt_ref, br_ref, bi_ref,
                    abr_ref, abi_ref, bbr_ref, bbi_ref):
    ar, ai = ar_ref[...], ai_ref[...]
    dt = jnp.exp(ldt_ref[...])
    mag = jnp.exp(dt * ar)
    ab_re = mag * jnp.cos(dt * ai)
    ab_im = mag * jnp.sin(dt * ai)
    den = ar * ar + ai * ai
    nr, ni = ab_re - 1.0, ab_im
    f_re = (nr * ar + ni * ai) / den
    f_im = (ni * ar - nr * ai) / den
    br, bi = br_ref[...], bi_ref[...]
    abr_ref[...] = ab_re
    abi_ref[...] = ab_im
    bbr_ref[...] = f_re * br - f_im * bi
    bbi_ref[...] = f_re * bi + f_im * br


def _s5_prep_call(a_re, a_im, log_dt, b_re, b_im):
    g, p = a_re.shape
    h = b_re.shape[-1]
    n = g * p
    row = lambda v: v.reshape(1, n)
    to_hn = lambda v: v.reshape(n, h).T
    vec = jax.ShapeDtypeStruct((1, n), F32)
    mat = jax.ShapeDtypeStruct((h, n), F32)
    return pl.pallas_call(
        _s5_prep_kernel,
        out_shape=(vec, vec, mat, mat),
        name="s5_prep",
    )(row(a_re), row(a_im), row(jnp.repeat(log_dt, p)), to_hn(b_re), to_hn(b_im))


def _swiglu_residual(x2, h2, gate2, w1_ref, w3_ref, w2_ref, alpha):
    hb = h2.astype(BF16)
    acc = None
    for lo, hi in _ffn_chunks(w1_ref.shape[1]):
        a = _dot(hb, w1_ref[:, lo:hi])
        b = _dot(hb, w3_ref[:, lo:hi])
        gg = (_silu(a) * b).astype(BF16)
        part = _dot(gg, w2_ref[lo:hi, :])
        acc = part if acc is None else acc + part
    return alpha * x2 + (0.5 * gate2) * acc


def _delayed_epilogue(step, n_tiles, compute, finish):
    @pl.when(step == 0)
    def _():
        compute()

    @pl.when(jnp.logical_and(step > 0, step < n_tiles))
    def _():
        finish()
        compute()

    @pl.when(step == n_tiles)
    def _():
        finish()


def _three_stage(step, n_tiles, front, compute, finish):
    @pl.when(step == 0)
    def _():
        front()

    @pl.when(step == 1)
    def _():
        compute()
        front()

    @pl.when(jnp.logical_and(step >= 2, step < n_tiles))
    def _():
        finish()
        compute()
        front()

    @pl.when(step == n_tiles)
    def _():
        finish()
        compute()

    @pl.when(step == n_tiles + 1)
    def _():
        finish()


def _ffn1_kernel(x_ref, mod_ref, w1_ref, w3_ref, w2_ref, ln_ref, o_ref, pend_ref, tmp_ref,
                 *, alpha, n_tiles):
    nb, tt, d = x_ref.shape

    def compute():
        sh, sc, gt = mod_ref[0], mod_ref[1], mod_ref[2]
        x3 = x_ref[...]
        h3 = x3 * (1.0 + sc)[:, None, :] + sh[:, None, :]
        gate3 = jnp.broadcast_to(gt[:, None, :], (nb, tt, d))
        pend_ref[...] = _swiglu_residual(x3.reshape(nb * tt, d), h3.reshape(nb * tt, d),
                                         gate3.reshape(nb * tt, d), w1_ref, w3_ref, w2_ref, alpha)

    def finish():
        out = _layer_norm(pend_ref[...], ln_ref[0:1, :], ln_ref[1:2, :])
        for j in range(d // LANES):
            for b in range(nb):
                tmp_ref[j, pl.ds(b, tt, stride=nb), :] = out[b * tt:(b + 1) * tt,
                                                             j * LANES:(j + 1) * LANES]
            o_ref[:, j * LANES:(j + 1) * LANES] = tmp_ref[j]

    _delayed_epilogue(pl.program_id(0), n_tiles, compute, finish)


def _ffn2_kernel(x_ref, y_ref, mod2_ref, mod3_ref, mod3_lag_ref, wo_ref, ln_ref,
                 w1_ref, w3_ref, w2_ref, o_ref, hb_buf, x2_buf, pend_ref, tmp_ref,
                 *, alpha, n_tiles):
    nb, tt, d = o_ref.shape

    def front():
        x1 = x_ref[...].reshape(tt, nb, d)
        m = _dot(y_ref[...], wo_ref[...]).reshape(tt, nb, d)
        x2 = _layer_norm((alpha * x1 + mod2_ref[2][None] * m).reshape(tt * nb, d),
                         ln_ref[2:3, :], ln_ref[3:4, :])
        sh, sc = mod3_ref[0], mod3_ref[1]
        h3 = x2.reshape(tt, nb, d) * (1.0 + sc)[None] + sh[None]
        x2_buf[...] = x2
        hb_buf[...] = h3.reshape(tt * nb, d).astype(BF16)

    def compute():
        gate3 = jnp.broadcast_to(mod3_lag_ref[2][None], (tt, nb, d))
        pend_ref[...] = _swiglu_residual(x2_buf[...], hb_buf[...], gate3.reshape(tt * nb, d),
                                         w1_ref, w3_ref, w2_ref, alpha)

    def finish():
        out = _layer_norm(pend_ref[...], ln_ref[4:5, :], ln_ref[5:6, :])
        for j in range(d // LANES):
            tmp_ref[j] = out[:, j * LANES:(j + 1) * LANES]
            for b in range(nb):
                o_ref[b, :, j * LANES:(j + 1) * LANES] = tmp_ref[j, pl.ds(b, tt, stride=nb), :]

    _three_stage(pl.program_id(0), n_tiles, front, compute, finish)


def _ffn_chunks(d_ff):
    assert d_ff % MXU_TILE == 0
    cut = (d_ff // MXU_TILE + 1) // 2 * MXU_TILE
    return ((0, cut), (cut, d_ff)) if cut < d_ff else ((0, d_ff),)


def _ffn_weights(w1, w3, w2):
    return w1.astype(BF16), w3.astype(BF16), w2.astype(BF16)


def _tile_maps(n_tiles, per_half, lag=1):
    def split(t):
        return t // per_half, t % per_half
    in_tile = lambda s: split(jnp.minimum(s, n_tiles - 1))
    out_tile = lambda s: split(jnp.clip(s - lag, 0, n_tiles - 1))
    return in_tile, out_tile


def _ffn1_call(x, mod1, w1, w3, w2, ln_vecs, alpha):
    bsz, s, d = x.shape
    nh = bsz // NB
    tt = FFN_TT
    per_half = s // tt
    n_tiles = nh * per_half
    in_tile, out_tile = _tile_maps(n_tiles, per_half)
    w1c, w3c, w2c = _ffn_weights(w1, w3, w2)
    return pl.pallas_call(
        functools.partial(_ffn1_kernel, alpha=alpha, n_tiles=n_tiles),
        grid=(n_tiles + 1,),
        in_specs=[pl.BlockSpec((NB, tt, d), lambda s: (*in_tile(s), 0)),
                  pl.BlockSpec((None, 3, NB, d), lambda s: (in_tile(s)[0], 0, 0, 0)),
                  _resident(w1c.shape), _resident(w3c.shape), _resident(w2c.shape),
                  _resident(ln_vecs.shape)],
        out_specs=pl.BlockSpec((None, tt * NB, d), lambda s: (*out_tile(s), 0)),
        out_shape=jax.ShapeDtypeStruct((nh, s * NB, d), F32),
        scratch_shapes=[pltpu.VMEM((tt * NB, d), F32),
                        pltpu.VMEM((d // LANES, tt * NB, LANES), F32)],
        compiler_params=pltpu.CompilerParams(
            dimension_semantics=("arbitrary",), vmem_limit_bytes=VMEM_LIMIT_BYTES),
        name="ffn1_ln1",
    )(x, mod1, w1c, w3c, w2c, ln_vecs)


def _ffn2_call(x1t, ycat, mod2, mod3, w_out, ln_vecs, w1, w3, w2, alpha):
    nh, rows, d = x1t.shape
    s = rows // NB
    tt = FFN_TT
    per_half = s // tt
    n_tiles = nh * per_half
    in_tile, out_tile = _tile_maps(n_tiles, per_half, lag=2)
    _, mid_tile = _tile_maps(n_tiles, per_half, lag=1)
    w1c, w3c, w2c = _ffn_weights(w1, w3, w2)
    tile_in = pl.BlockSpec((None, tt * NB, d), lambda s: (*in_tile(s), 0))
    mod_in = pl.BlockSpec((None, 3, NB, d), lambda s: (in_tile(s)[0], 0, 0, 0))
    mod_lag = pl.BlockSpec((None, 3, NB, d), lambda s: (mid_tile(s)[0], 0, 0, 0))
    return pl.pallas_call(
        functools.partial(_ffn2_kernel, alpha=alpha, n_tiles=n_tiles),
        grid=(n_tiles + 2,),
        in_specs=[tile_in, tile_in, mod_in, mod_in, mod_lag,
                  _resident(w_out.shape), _resident(ln_vecs.shape),
                  _resident(w1c.shape), _resident(w3c.shape), _resident(w2c.shape)],
        out_specs=pl.BlockSpec((NB, tt, d), lambda s: (*out_tile(s), 0)),
        out_shape=jax.ShapeDtypeStruct((nh * NB, s, d), F32),
        scratch_shapes=[pltpu.VMEM((tt * NB, d), BF16),
                        pltpu.VMEM((tt * NB, d), F32),
                        pltpu.VMEM((tt * NB, d), F32),
                        pltpu.VMEM((d // LANES, tt * NB, LANES), F32)],
        compiler_params=pltpu.CompilerParams(
            dimension_semantics=("arbitrary",), vmem_limit_bytes=VMEM_LIMIT_BYTES),
        name="outproj_ln2_ffn2_ln3",
    )(x1t, ycat, mod2, mod3, mod3, w_out.astype(BF16), ln_vecs, w1c, w3c, w2c)


def _mixer_kernel(x_ref, mod_ref, win_ref, cw_ref, vec_ref, hexp3_ref,
                  abr_ref, abi_ref, bmat_ref, cmat_ref, wg_ref,
                  o_ref,
                  z_buf, xbc_buf, act_buf, u_buf, acs_buf, y_buf, cdec_buf,
                  hstate, bu0_buf, bu1_buf, xb0_buf, xb1_buf, y5_buf, s5_state,
                  *, n_heads, n_groups):
    rows, d = x_ref.shape
    nb = NB
    tc = rows // nb
    halo = (SSD_CONV - 1) * nb
    ssd_w = z_buf.shape[1]
    conv_ch = xbc_buf.shape[0] * LANES
    xp = ssd_w // LANES
    assert SSD_STATE == LANES
    hpg = n_heads // n_groups
    gw = hpg * SSD_HEAD_DIM
    n_half, hk = bmat_ref.shape[0], bmat_ref.shape[1]
    sk = bmat_ref.shape[2] // 2
    s5_w = n_half * hk
    c_idx = pl.program_id(1)
    o_x, o_u, o_dt = ssd_w, ssd_w + conv_ch, ssd_w + conv_ch + s5_w
    cb_row = vec_ref[0:1, :]
    nw_row, s5d_row = vec_ref[1:2, 0:ssd_w], vec_ref[1:2, ssd_w:ssd_w + s5_w]
    dfull_row, bg_row = vec_ref[2:3, 0:ssd_w], vec_ref[2:3, ssd_w:ssd_w + s5_w]
    dtb_row, alog_row = vec_ref[3:4, 0:LANES], vec_ref[3:4, LANES:2 * LANES]

    @pl.when(c_idx == 0)
    def _():
        xbc_buf[:, 0:halo, :] = jnp.zeros((conv_ch // LANES, halo, LANES), F32)
        hstate[...] = jnp.zeros(hstate.shape, F32)
        s5_state[...] = jnp.zeros(s5_state.shape, F32)

    sh, sc = mod_ref[0], mod_ref[1]
    hb = (x_ref[...].reshape(tc, nb, d) * (1.0 + sc)[None] + sh[None]).reshape(rows, d).astype(BF16)
    xbc = _dot(hb, win_ref[:, o_x:o_u])
    for j in range(conv_ch // LANES):
        xbc_buf[j, halo:halo + rows, :] = xbc[:, j * LANES:(j + 1) * LANES]
    dt_raw = _dot(hb, win_ref[:, o_dt:o_dt + LANES])
    for k in range(n_half):
        u_buf[k] = _dot(hb, win_ref[:, o_u + k * hk:o_u + (k + 1) * hk])
    z_buf[...] = _dot(hb, win_ref[:, 0:o_x])

    for j in range(conv_ch // LANES):
        cs = slice(j * LANES, (j + 1) * LANES)
        conv = cb_row[:, cs]
        for k in range(SSD_CONV):
            conv = conv + cw_ref[k:k + 1, cs] * xbc_buf[j, k * nb:k * nb + rows, :]
        act_buf[j] = _silu(conv)
        xbc_buf[j, 0:halo, :] = xbc_buf[j, rows:rows + halo, :]

    dt = _softplus(dt_raw + dtb_row)
    acs = dt * (-jnp.exp(alog_row))
    shift = nb
    while shift < rows:
        acs = acs + jnp.concatenate([jnp.zeros((shift, LANES), F32), acs[:rows - shift]], axis=0)
        shift *= 2
    acs_buf[...] = acs
    hi = acs.astype(BF16)
    r1 = acs - hi.astype(F32)
    mid = r1.astype(BF16)
    lo = (r1 - mid.astype(F32)).astype(BF16)
    acs_full = _dot(jnp.concatenate([hi, mid, lo], axis=-1), hexp3_ref[...])
    dt_full = _dot(dt.astype(BF16), hexp3_ref[0:LANES, :])
    last = acs_full[rows - nb:rows, :]
    cdec_buf[...] = jnp.exp(last)
    dec = jnp.exp(last[None] - acs_full.reshape(tc, nb, ssd_w)).reshape(rows, ssd_w)
    eac = jnp.exp(acs_full)
    for j in range(xp):
        cs = slice(j * LANES, (j + 1) * LANES)
        xs = act_buf[j]
        xin = xs * dt_full[:, cs]
        y_buf[j] = dfull_row[:, cs] * xs
        act_buf[j] = xin
        xbc_buf[j, halo:halo + rows, :] = xin * dec[:, cs]
        xbc_buf[xp + j, halo:halo + rows, :] = eac[:, cs]

    causal = (lax.broadcasted_iota(jnp.int32, (tc, tc), 0)
              >= lax.broadcasted_iota(jnp.int32, (tc, tc), 1))
    gp = gw // LANES
    lane_head = lax.broadcasted_iota(jnp.int32, (tc, gw), 1) // SSD_HEAD_DIM

    def ssd_group(b, g, rsel, acs_b, acs_t, cdec):
        gather = lambda buf, p0, rs: jnp.concatenate(
            [buf[p0 + g * gp + jj, rs, :] for jj in range(gp)], axis=-1)
        rsel_h = pl.ds(halo + b, tc, stride=nb)
        xin_g = gather(act_buf, 0, rsel)
        xdec_g, eacs_g = gather(xbc_buf, 0, rsel_h), gather(xbc_buf, xp, rsel_h)
        bm = act_buf[xp + g, rsel, :]
        cm_b = act_buf[xp + n_groups + g, rsel, :].astype(BF16)
        cbm = lax.dot_general(cm_b, bm.astype(BF16), (((1,), (1,)), ((), ())),
                              preferred_element_type=F32)
        h_prev = hstate[b, g]
        y_off = _dot(cm_b, h_prev.astype(BF16)) * eacs_g
        mms, xblk = [], []
        for z in range(hpg):
            hd = g * hpg + z
            seg = acs_b[:, hd:hd + 1] - acs_t[hd:hd + 1, :]
            lm = jnp.where(causal, jnp.exp(seg), 0.0)
            mms.append((cbm * lm).astype(BF16))
            xblk.append(jnp.where(lane_head == z, xin_g, 0.0).astype(BF16))
        y_g = _dot(jnp.concatenate(mms, axis=-1), jnp.concatenate(xblk, axis=0)) + y_off
        for jj in range(gp):
            y_buf[g * gp + jj, rsel, :] = (y_buf[g * gp + jj, rsel, :]
                                           + y_g[:, jj * LANES:(jj + 1) * LANES])
        st = _dot(bm.T.astype(BF16), xdec_g.astype(BF16))
        hstate[b, g] = cdec[:, g * gw:(g + 1) * gw] * h_prev + st

    def ssd_one(b):
        rsel = pl.ds(b, tc, stride=nb)
        acs_b = acs_buf[rsel, :]
        acs_t = acs_b.T
        cdec = cdec_buf[pl.ds(b, 1), :]
        for g in range(n_groups):
            ssd_group(b, g, rsel, acs_b, acs_t, cdec)

    sub_rows = S5_SUB_T * nb
    n_units = (tc // S5_SUB_T) * n_half
    assert n_units == nb and n_half == 2
    bu_bufs, xb_bufs = (bu0_buf, bu1_buf), (xb0_buf, xb1_buf)

    def unit_rows(i):
        start = (i // n_half) * sub_rows
        return pl.ds(start if isinstance(i, int) else pl.multiple_of(start, sub_rows), sub_rows)

    def s5_in(i, k):
        bu_bufs[k][...] = _dot(u_buf[k, unit_rows(i), :].astype(BF16), bmat_ref[k])

    def s5_scan(k):
        bu, xb = bu_bufs[k], xb_bufs[k]
        a_r, a_i = abr_ref[k], abi_ref[k]
        xr, xi = s5_state[k, 0], s5_state[k, 1]
        for t in range(0, S5_SUB_T, 2):
            res_r, res_i = [], []
            for tt in (t, t + 1):
                rr = slice(tt * nb, (tt + 1) * nb)
                xr, xi = (a_r * xr - a_i * xi + bu[rr, 0:sk],
                          a_r * xi + a_i * xr + bu[rr, sk:2 * sk])
                res_r.append(xr)
                res_i.append(xi)
            r2 = slice(t * nb, (t + 2) * nb)
            xb[r2, 0:sk] = jnp.concatenate(res_r, axis=0).astype(BF16)
            xb[r2, sk:2 * sk] = jnp.concatenate(res_i, axis=0).astype(BF16)
        s5_state[k, 0] = xr
        s5_state[k, 1] = xi

    def s5_out(i, k):
        y5_buf[k, unit_rows(i), :] = _dot(xb_bufs[k][...], cmat_ref[k])

    def stage(i, k, first=False, last=False):
        s5_scan(k)
        if not last:
            s5_in(i + 1, 1 - k)
        if not first:
            s5_out(i - 1, 1 - k)
        ssd_one(i)

    s5_in(0, 0)
    for i in range(n_units):
        stage(i, i % n_half, first=i == 0, last=i == n_units - 1)
    s5_out(n_units - 1, (n_units - 1) % n_half)

    y = jnp.concatenate([y_buf[j] for j in range(xp)], axis=-1)
    y = y * _silu(z_buf[...])
    nw = ssd_w // n_groups
    parts = []
    for g in range(n_groups):
        yg = y[:, g * nw:(g + 1) * nw]
        parts.append(yg * lax.rsqrt(jnp.mean(yg * yg, axis=-1, keepdims=True) + LN_EPS))
    o_ref[:, 0:ssd_w] = (jnp.concatenate(parts, axis=-1) * nw_row).astype(o_ref.dtype)

    y5 = jnp.concatenate([y5_buf[k] + u_buf[k] * s5d_row[:, k * hk:(k + 1) * hk]
                          for k in range(n_half)], axis=-1)
    gl = jax.nn.gelu(y5, approximate=True)
    gate = _sigmoid(_dot(gl.astype(BF16), wg_ref[...]) + bg_row)
    o_ref[:, ssd_w:ssd_w + s5_w] = (gl * gate).astype(o_ref.dtype)


def _mixer_call(x1t, mod2, w_in, conv_w, conv_b, dt_bias, a_log, d_ssd, ssd_norm_w,
                ab_re, ab_im, bmat, cmat, s5_d, w_glu, b_glu):
    nh, rows_total, d = x1t.shape
    s = rows_total // NB
    ssd_w = ssd_norm_w.shape[0]
    n_heads = d_ssd.shape[0]
    conv_ch = conv_w.shape[1]
    s5_w = s5_d.shape[0]
    n_groups = SSD_GROUPS
    tc = SSD_CHUNK
    rows = tc * NB
    halo = (SSD_CONV - 1) * NB

    assert conv_ch == d and ssd_w + s5_w == d
    o1, o2, o3 = ssd_w, ssd_w + conv_ch, ssd_w + conv_ch + n_heads
    w_all = jnp.concatenate([w_in[:, :o2], w_in[:, o3:], w_in[:, o2:o3],
                             jnp.zeros((d, LANES - n_heads), w_in.dtype)], axis=1).astype(BF16)
    lane_pad = lambda v: jnp.pad(v, (0, LANES - n_heads))
    vecs = jnp.stack([conv_b,
                      jnp.concatenate([ssd_norm_w, s5_d]),
                      jnp.concatenate([jnp.repeat(d_ssd, SSD_HEAD_DIM), b_glu]),
                      jnp.concatenate([lane_pad(dt_bias), lane_pad(a_log),
                                       jnp.zeros((d - 2 * LANES,), F32)])])
    hexp = (jnp.arange(LANES)[:, None] == (jnp.arange(ssd_w) // SSD_HEAD_DIM)[None, :]).astype(BF16)
    hexp = jnp.concatenate([hexp] * 3, axis=0)

    n_half, hk, sk2 = bmat.shape
    kern = functools.partial(_mixer_kernel, n_heads=n_heads, n_groups=n_groups)
    xmap = lambda h, c: (h, c, 0)
    return pl.pallas_call(
        kern,
        grid=(nh, s // tc),
        in_specs=[pl.BlockSpec((None, rows, d), xmap),
                  pl.BlockSpec((None, 3, NB, d), lambda h, c: (h, 0, 0, 0)),
                  _resident(w_all.shape), _resident(conv_w.shape), _resident(vecs.shape),
                  _resident(hexp.shape),
                  _resident(ab_re.shape), _resident(ab_im.shape),
                  _resident(bmat.shape), _resident(cmat.shape), _resident(w_glu.shape)],
        out_specs=pl.BlockSpec((None, rows, ssd_w + s5_w), xmap),
        out_shape=jax.ShapeDtypeStruct((nh, rows_total, ssd_w + s5_w), BF16),
        scratch_shapes=[
            pltpu.VMEM((rows, ssd_w), F32),
            pltpu.VMEM((conv_ch // LANES, rows + halo, LANES), F32),
            pltpu.VMEM((conv_ch // LANES, rows, LANES), F32),
            pltpu.VMEM((n_half, rows, hk), F32),
            pltpu.VMEM((rows, LANES), F32),
            pltpu.VMEM((ssd_w // LANES, rows, LANES), F32),
            pltpu.VMEM((NB, ssd_w), F32),
            pltpu.VMEM((NB, n_groups, SSD_STATE, ssd_w // n_groups), F32),
            pltpu.VMEM((S5_SUB_T * NB, sk2), F32),
            pltpu.VMEM((S5_SUB_T * NB, sk2), F32),
            pltpu.VMEM((S5_SUB_T * NB, sk2), BF16),
            pltpu.VMEM((S5_SUB_T * NB, sk2), BF16),
            pltpu.VMEM((n_half, rows, hk), F32),
            pltpu.VMEM((n_half, 2, NB, sk2 // 2), F32),
        ],
        compiler_params=pltpu.CompilerParams(
            dimension_semantics=("arbitrary", "arbitrary"), vmem_limit_bytes=VMEM_LIMIT_BYTES),
        name="mixer",
    )(x1t, mod2, w_all, conv_w, vecs, hexp, ab_re, ab_im, bmat, cmat, w_glu.astype(BF16))


def _s5_matrices(ab_re, ab_im, bb_re, bb_im, c_re, c_im):
    h, n = bb_re.shape
    g, _, p = c_re.shape
    n_half = 2
    gh = g // n_half
    eye = jnp.eye(gh, dtype=F32)

    def b_block(bb):
        v = bb.reshape(h, n_half, gh, p)
        return jnp.einsum("hkgp,gG->kghGp", v, eye).reshape(n_half, gh * h, gh * p)

    def c_block(c):
        v = c.reshape(n_half, gh, h, p)
        return jnp.einsum("kghp,gG->kgpGh", v, eye).reshape(n_half, gh * p, gh * h)

    bmat = jnp.concatenate([b_block(bb_re), b_block(bb_im)], axis=-1).astype(BF16)
    cmat = jnp.concatenate([c_block(c_re), c_block(-c_im)], axis=1).astype(BF16)
    bc = lambda v: jnp.broadcast_to(v.reshape(n_half, 1, n // n_half), (n_half, NB, n // n_half))
    return bc(ab_re), bc(ab_im), bmat, cmat


def kernel(x, c, w_ada, b_ada, ffn1_w1, ffn1_w3, ffn1_w2, ln1_g, ln1_b, w_in, conv_w, conv_b,
           dt_bias, a_log, d_ssd, ssd_norm_w, s5_a_re, s5_a_im, s5_log_dt, s5_b_re, s5_b_im,
           s5_c_re, s5_c_im, s5_d, w_glu, b_glu, w_out, ln2_g, ln2_b, ffn2_w1, ffn2_w3, ffn2_w2,
           ln3_g, ln3_b):
    bsz, s, d = x.shape
    depth = w_ada.shape[0]
    alpha = (2 * depth) ** 0.25
    assert depth == 1
    assert bsz % NB == 0 and s % SSD_CHUNK == 0 and s % FFN_TT == 0
    nh = bsz // NB
    for l in range(depth):
        mod = _ada_call(c, w_ada[l], b_ada[l]).reshape(bsz, N_MOD // 3, 3, d)
        mod = mod.reshape(nh, NB, N_MOD // 3, 3, d).transpose(2, 0, 3, 1, 4)
        ab_re, ab_im, bb_re, bb_im = _s5_prep_call(s5_a_re[l], s5_a_im[l], s5_log_dt[l],
                                                   s5_b_re[l], s5_b_im[l])
        ab_re, ab_im, bmat, cmat = _s5_matrices(ab_re, ab_im, bb_re, bb_im, s5_c_re[l], s5_c_im[l])
        ln_vecs = jnp.stack([ln1_g[l], ln1_b[l], ln2_g[l], ln2_b[l], ln3_g[l], ln3_b[l]])
        x1t = _ffn1_call(x, mod[0], ffn1_w1[l], ffn1_w3[l], ffn1_w2[l], ln_vecs, alpha)
        ycat = _mixer_call(x1t, mod[1], w_in[l], conv_w[l], conv_b[l], dt_bias[l], a_log[l],
                           d_ssd[l], ssd_norm_w[l], ab_re, ab_im, bmat, cmat, s5_d[l],
                           w_glu[l], b_glu[l])
        x = _ffn2_call(x1t, ycat, mod[1], mod[2], w_out[l], ln_vecs,
                       ffn2_w1[l], ffn2_w3[l], ffn2_w2[l], alpha)
    return x
```

```python
import functools

import jax
import jax.numpy as jnp
from jax import lax
from jax.experimental import pallas as pl
from jax.experimental.pallas import tpu as pltpu

F32 = jnp.float32
BF16 = jnp.bfloat16

LN_EPS = 1e-5
SUBLANES = 8
LANES = 128
MXU_TILE = 256
VMEM_LIMIT_BYTES = 56 * 1024 * 1024

SSD_HEAD_DIM = 64
SSD_GROUPS = 2
SSD_STATE = 128
SSD_CONV = 4
SSD_CHUNK = 128
N_MOD = 9

NB = SUBLANES
FFN_TT = 64
S5_SUB_T = 32
TAIL_ROWS = 256


def _dot(a, b):
    return jnp.dot(a, b, preferred_element_type=F32)


def _sigmoid(x):
    return 1.0 / (1.0 + jnp.exp(-x))


def _silu(x):
    return x * _sigmoid(x)


def _softplus(x):
    return jnp.maximum(x, 0.0) + jnp.log1p(jnp.exp(-jnp.abs(x)))


def _layer_norm(y, g, b):
    mu = jnp.mean(y, axis=-1, keepdims=True)
    yc = y - mu
    var = jnp.mean(yc * yc, axis=-1, keepdims=True)
    return yc * lax.rsqrt(var + LN_EPS) * g + b


def _resident(shape):
    nd = len(shape)
    return pl.BlockSpec(shape, lambda *_: (0,) * nd, pipeline_mode=pl.Buffered(1))


def _ada_kernel(c_ref, w_ref, b_ref, o_ref):
    cs = _silu(c_ref[...]).astype(BF16)
    o_ref[...] = _dot(cs, w_ref[...].astype(BF16)) + b_ref[...]


def _ada_call(c, w_ada, b_ada):
    bsz, d = c.shape
    n = w_ada.shape[1]
    bn = n // 8
    return pl.pallas_call(
        _ada_kernel,
        grid=(n // bn,),
        in_specs=[pl.BlockSpec((bsz, d), lambda j: (0, 0)),
                  pl.BlockSpec((d, bn), lambda j: (0, j)),
                  pl.BlockSpec((1, bn), lambda j: (0, j))],
        out_specs=pl.BlockSpec((bsz, bn), lambda j: (0, j)),
        out_shape=jax.ShapeDtypeStruct((bsz, n), F32),
        compiler_params=pltpu.CompilerParams(dimension_semantics=("arbitrary",)),
        name="ada_mod",
    )(c, w_ada, b_ada.reshape(1, n))


def _s5_prep_kernel(ar_ref, ai_ref, ldt_ref, br_ref, bi_ref,
                    abr_ref, abi_ref, bbr_ref, bbi_ref):
    ar, ai = ar_ref[...], ai_ref[...]
    dt = jnp.exp(ldt_ref[...])
    mag = jnp.exp(dt * ar)
    ab_re = mag * jnp.cos(dt * ai)
    ab_im = mag * jnp.sin(dt * ai)
    den = ar * ar + ai * ai
    nr, ni = ab_re - 1.0, ab_im
    f_re = (nr * ar + ni * ai) / den
    f_im = (ni * ar - nr * ai) / den
    br, bi = br_ref[...], bi_ref[...]
    abr_ref[...] = ab_re
    abi_ref[...] = ab_im
    bbr_ref[...] = f_re * br - f_im * bi
    bbi_ref[...] = f_re * bi + f_im * br


def _s5_prep_call(a_re, a_im, log_dt, b_re, b_im):
    g, p = a_re.shape
    h = b_re.shape[-1]
    n = g * p
    row = lambda v: v.reshape(1, n)
    to_hn = lambda v: v.reshape(n, h).T
    vec = jax.ShapeDtypeStruct((1, n), F32)
    mat = jax.ShapeDtypeStruct((h, n), F32)
    return pl.pallas_call(
        _s5_prep_kernel,
        out_shape=(vec, vec, mat, mat),
        name="s5_prep",
    )(row(a_re), row(a_im), row(jnp.repeat(log_dt, p)), to_hn(b_re), to_hn(b_im))


def _swiglu_residual(x2, h2, gate2, w1_ref, w3_ref, w2_ref, alpha):
    hb = h2.astype(BF16)
    acc = None
    for lo, hi in _ffn_chunks(w1_ref.shape[1]):
        a = _dot(hb, w1_ref[:, lo:hi])
        b = _dot(hb, w3_ref[:, lo:hi])
        gg = (_silu(a) * b).astype(BF16)
        part = _dot(gg, w2_ref[lo:hi, :])
        acc = part if acc is None else acc + part
    return alpha * x2 + (0.5 * gate2) * acc


def _delayed_epilogue(step, n_tiles, compute, finish):
    @pl.when(step == 0)
    def _():
        compute()

    @pl.when(jnp.logical_and(step > 0, step < n_tiles))
    def _():
        finish()
        compute()

    @pl.when(step == n_tiles)
    def _():
        finish()


def _ffn1_kernel(x_ref, mod_ref, w1_ref, w3_ref, w2_ref, ln_ref, o_ref, pend_ref, tmp_ref,
                 *, alpha, n_tiles):
    nb, tt, d = x_ref.shape

    def compute():
        sh, sc, gt = mod_ref[0], mod_ref[1], mod_ref[2]
        x3 = x_ref[...]
        h3 = x3 * (1.0 + sc)[:, None, :] + sh[:, None, :]
        gate3 = jnp.broadcast_to(gt[:, None, :], (nb, tt, d))
        pend_ref[...] = _swiglu_residual(x3.reshape(nb * tt, d), h3.reshape(nb * tt, d),
                                         gate3.reshape(nb * tt, d), w1_ref, w3_ref, w2_ref, alpha)

    def finish():
        out = _layer_norm(pend_ref[...], ln_ref[0:1, :], ln_ref[1:2, :])
        for j in range(d // LANES):
            for b in range(nb):
                tmp_ref[j, pl.ds(b, tt, stride=nb), :] = out[b * tt:(b + 1) * tt,
                                                             j * LANES:(j + 1) * LANES]
            o_ref[:, j * LANES:(j + 1) * LANES] = tmp_ref[j]

    _delayed_epilogue(pl.program_id(0), n_tiles, compute, finish)


def _ffn2_kernel(x_ref, m_ref, mod2_ref, mod3_ref, ln_ref,
                 w1_ref, w3_ref, w2_ref, o_ref, pend_ref, tmp_ref,
                 *, alpha, n_tiles):
    nb, tt, d = o_ref.shape

    def compute():
        x1 = x_ref[...].reshape(tt, nb, d)
        m = m_ref[...].astype(F32).reshape(tt, nb, d)
        x2 = _layer_norm((alpha * x1 + mod2_ref[2][None] * m).reshape(tt * nb, d),
                         ln_ref[2:3, :], ln_ref[3:4, :])
        sh, sc, gt = mod3_ref[0], mod3_ref[1], mod3_ref[2]
        h3 = x2.reshape(tt, nb, d) * (1.0 + sc)[None] + sh[None]
        gate3 = jnp.broadcast_to(gt[None], (tt, nb, d))
        pend_ref[...] = _swiglu_residual(x2, h3.reshape(tt * nb, d), gate3.reshape(tt * nb, d),
                                         w1_ref, w3_ref, w2_ref, alpha)

    def finish():
        out = _layer_norm(pend_ref[...], ln_ref[4:5, :], ln_ref[5:6, :])
        for j in range(d // LANES):
            tmp_ref[j] = out[:, j * LANES:(j + 1) * LANES]
            for b in range(nb):
                o_ref[b, :, j * LANES:(j + 1) * LANES] = tmp_ref[j, pl.ds(b, tt, stride=nb), :]

    _delayed_epilogue(pl.program_id(0), n_tiles, compute, finish)


def _ffn_chunks(d_ff):
    assert d_ff % MXU_TILE == 0
    cut = (d_ff // MXU_TILE + 1) // 2 * MXU_TILE
    return ((0, cut), (cut, d_ff)) if cut < d_ff else ((0, d_ff),)


def _ffn_weights(w1, w3, w2):
    return w1.astype(BF16), w3.astype(BF16), w2.astype(BF16)


def _tile_maps(n_tiles, per_half):
    def split(t):
        return t // per_half, t % per_half
    in_tile = lambda s: split(jnp.minimum(s, n_tiles - 1))
    out_tile = lambda s: split(jnp.maximum(s - 1, 0))
    return in_tile, out_tile


def _ffn1_call(x, mod1, w1, w3, w2, ln_vecs, alpha):
    bsz, s, d = x.shape
    nh = bsz // NB
    tt = FFN_TT
    per_half = s // tt
    n_tiles = nh * per_half
    in_tile, out_tile = _tile_maps(n_tiles, per_half)
    w1c, w3c, w2c = _ffn_weights(w1, w3, w2)
    return pl.pallas_call(
        functools.partial(_ffn1_kernel, alpha=alpha, n_tiles=n_tiles),
        grid=(n_tiles + 1,),
        in_specs=[pl.BlockSpec((NB, tt, d), lambda s: (*in_tile(s), 0)),
                  pl.BlockSpec((None, 3, NB, d), lambda s: (in_tile(s)[0], 0, 0, 0)),
                  _resident(w1c.shape), _resident(w3c.shape), _resident(w2c.shape),
                  _resident(ln_vecs.shape)],
        out_specs=pl.BlockSpec((None, tt * NB, d), lambda s: (*out_tile(s), 0)),
        out_shape=jax.ShapeDtypeStruct((nh, s * NB, d), F32),
        scratch_shapes=[pltpu.VMEM((tt * NB, d), F32),
                        pltpu.VMEM((d // LANES, tt * NB, LANES), F32)],
        compiler_params=pltpu.CompilerParams(
            dimension_semantics=("arbitrary",), vmem_limit_bytes=VMEM_LIMIT_BYTES),
        name="ffn1_ln1",
    )(x, mod1, w1c, w3c, w2c, ln_vecs)


def _ffn2_call(x1t, mix, mod2, mod3, ln_vecs, w1, w3, w2, alpha):
    nh, rows, d = x1t.shape
    s = rows // NB
    tt = FFN_TT
    per_half = s // tt
    n_tiles = nh * per_half
    in_tile, out_tile = _tile_maps(n_tiles, per_half)
    w1c, w3c, w2c = _ffn_weights(w1, w3, w2)
    tile_in = pl.BlockSpec((None, tt * NB, d), lambda s: (*in_tile(s), 0))
    mod_in = pl.BlockSpec((None, 3, NB, d), lambda s: (in_tile(s)[0], 0, 0, 0))
    return pl.pallas_call(
        functools.partial(_ffn2_kernel, alpha=alpha, n_tiles=n_tiles),
        grid=(n_tiles + 1,),
        in_specs=[tile_in, tile_in, mod_in, mod_in,
                  _resident(ln_vecs.shape),
                  _resident(w1c.shape), _resident(w3c.shape), _resident(w2c.shape)],
        out_specs=pl.BlockSpec((NB, tt, d), lambda s: (*out_tile(s), 0)),
        out_shape=jax.ShapeDtypeStruct((nh * NB, s, d), F32),
        scratch_shapes=[pltpu.VMEM((tt * NB, d), F32),
                        pltpu.VMEM((d // LANES, tt * NB, LANES), F32)],
        compiler_params=pltpu.CompilerParams(
            dimension_semantics=("arbitrary",), vmem_limit_bytes=VMEM_LIMIT_BYTES),
        name="ln2_ffn2_ln3",
    )(x1t, mix, mod2, mod3, ln_vecs, w1c, w3c, w2c)


def _mixer_kernel(x_ref, mod_ref, win_ref, cw_ref, vec_ref, hexp3_ref,
                  abr_ref, abi_ref, bmat_ref, cmat_ref, wg_ref, wo_ref,
                  o_ref,
                  z_buf, xbc_buf, act_buf, u_buf, acs_buf, y_buf, cdec_buf,
                  hstate, bu0_buf, bu1_buf, xb0_buf, xb1_buf, y5_buf, s5_state,
                  *, n_heads, n_groups):
    rows, d = x_ref.shape
    nb = NB
    tc = rows // nb
    halo = (SSD_CONV - 1) * nb
    ssd_w = z_buf.shape[1]
    conv_ch = xbc_buf.shape[0] * LANES
    xp = ssd_w // LANES
    assert SSD_STATE == LANES
    hpg = n_heads // n_groups
    gw = hpg * SSD_HEAD_DIM
    n_half, hk = bmat_ref.shape[0], bmat_ref.shape[1]
    sk = bmat_ref.shape[2] // 2
    s5_w = n_half * hk
    c_idx = pl.program_id(1)
    o_x, o_u, o_dt = ssd_w, ssd_w + conv_ch, ssd_w + conv_ch + s5_w
    cb_row = vec_ref[0:1, :]
    nw_row, s5d_row = vec_ref[1:2, 0:ssd_w], vec_ref[1:2, ssd_w:ssd_w + s5_w]
    dfull_row, bg_row = vec_ref[2:3, 0:ssd_w], vec_ref[2:3, ssd_w:ssd_w + s5_w]
    dtb_row, alog_row = vec_ref[3:4, 0:LANES], vec_ref[3:4, LANES:2 * LANES]

    @pl.when(c_idx == 0)
    def _():
        xbc_buf[:, 0:halo, :] = jnp.zeros((conv_ch // LANES, halo, LANES), F32)
        hstate[...] = jnp.zeros(hstate.shape, F32)
        s5_state[...] = jnp.zeros(s5_state.shape, F32)

    sh, sc = mod_ref[0], mod_ref[1]
    hb = (x_ref[...].reshape(tc, nb, d) * (1.0 + sc)[None] + sh[None]).reshape(rows, d).astype(BF16)
    xbc = _dot(hb, win_ref[:, o_x:o_u])
    for j in range(conv_ch // LANES):
        xbc_buf[j, halo:halo + rows, :] = xbc[:, j * LANES:(j + 1) * LANES]
    dt_raw = _dot(hb, win_ref[:, o_dt:o_dt + LANES])
    for k in range(n_half):
        u_buf[k] = _dot(hb, win_ref[:, o_u + k * hk:o_u + (k + 1) * hk])
    z_buf[...] = _dot(hb, win_ref[:, 0:o_x])

    for j in range(conv_ch // LANES):
        cs = slice(j * LANES, (j + 1) * LANES)
        conv = cb_row[:, cs]
        for k in range(SSD_CONV):
            conv = conv + cw_ref[k:k + 1, cs] * xbc_buf[j, k * nb:k * nb + rows, :]
        act_buf[j] = _silu(conv)
        xbc_buf[j, 0:halo, :] = xbc_buf[j, rows:rows + halo, :]

    dt = _softplus(dt_raw + dtb_row)
    acs = dt * (-jnp.exp(alog_row))
    shift = nb
    while shift < rows:
        acs = acs + jnp.concatenate([jnp.zeros((shift, LANES), F32), acs[:rows - shift]], axis=0)
        shift *= 2
    acs_buf[...] = acs
    hi = acs.astype(BF16)
    r1 = acs - hi.astype(F32)
    mid = r1.astype(BF16)
    lo = (r1 - mid.astype(F32)).astype(BF16)
    acs_full = _dot(jnp.concatenate([hi, mid, lo], axis=-1), hexp3_ref[...])
    dt_full = _dot(dt.astype(BF16), hexp3_ref[0:LANES, :])
    last = acs_full[rows - nb:rows, :]
    cdec_buf[...] = jnp.exp(last)
    dec = jnp.exp(last[None] - acs_full.reshape(tc, nb, ssd_w)).reshape(rows, ssd_w)
    eac = jnp.exp(acs_full)
    for j in range(xp):
        cs = slice(j * LANES, (j + 1) * LANES)
        xs = act_buf[j]
        xin = xs * dt_full[:, cs]
        y_buf[j] = dfull_row[:, cs] * xs
        act_buf[j] = xin
        xbc_buf[j, halo:halo + rows, :] = xin * dec[:, cs]
        xbc_buf[xp + j, halo:halo + rows, :] = eac[:, cs]

    causal = (lax.broadcasted_iota(jnp.int32, (tc, tc), 0)
              >= lax.broadcasted_iota(jnp.int32, (tc, tc), 1))
    gp = gw // LANES
    lane_head = lax.broadcasted_iota(jnp.int32, (tc, gw), 1) // SSD_HEAD_DIM

    def ssd_group(b, g, rsel, acs_b, acs_t, cdec):
        gather = lambda buf, p0, rs: jnp.concatenate(
            [buf[p0 + g * gp + jj, rs, :] for jj in range(gp)], axis=-1)
        rsel_h = pl.ds(halo + b, tc, stride=nb)
        xin_g = gather(act_buf, 0, rsel)
        xdec_g, eacs_g = gather(xbc_buf, 0, rsel_h), gather(xbc_buf, xp, rsel_h)
        bm = act_buf[xp + g, rsel, :]
        cm_b = act_buf[xp + n_groups + g, rsel, :].astype(BF16)
        cbm = lax.dot_general(cm_b, bm.astype(BF16), (((1,), (1,)), ((), ())),
                              preferred_element_type=F32)
        h_prev = hstate[b, g]
        y_off = _dot(cm_b, h_prev.astype(BF16)) * eacs_g
        mms, xblk = [], []
        for z in range(hpg):
            hd = g * hpg + z
            seg = acs_b[:, hd:hd + 1] - acs_t[hd:hd + 1, :]
            lm = jnp.where(causal, jnp.exp(seg), 0.0)
            mms.append((cbm * lm).astype(BF16))
            xblk.append(jnp.where(lane_head == z, xin_g, 0.0).astype(BF16))
        y_g = _dot(jnp.concatenate(mms, axis=-1), jnp.concatenate(xblk, axis=0)) + y_off
        for jj in range(gp):
            y_buf[g * gp + jj, rsel, :] = (y_buf[g * gp + jj, rsel, :]
                                           + y_g[:, jj * LANES:(jj + 1) * LANES])
        st = _dot(bm.T.astype(BF16), xdec_g.astype(BF16))
        hstate[b, g] = cdec[:, g * gw:(g + 1) * gw] * h_prev + st

    def ssd_one(b):
        rsel = pl.ds(b, tc, stride=nb)
        acs_b = acs_buf[rsel, :]
        acs_t = acs_b.T
        cdec = cdec_buf[pl.ds(b, 1), :]
        for g in range(n_groups):
            ssd_group(b, g, rsel, acs_b, acs_t, cdec)

    sub_rows = S5_SUB_T * nb
    n_units = (tc // S5_SUB_T) * n_half
    assert n_units == nb and n_half == 2
    bu_bufs, xb_bufs = (bu0_buf, bu1_buf), (xb0_buf, xb1_buf)

    def unit_rows(i):
        start = (i // n_half) * sub_rows
        return pl.ds(start if isinstance(i, int) else pl.multiple_of(start, sub_rows), sub_rows)

    def s5_in(i, k):
        bu_bufs[k][...] = _dot(u_buf[k, unit_rows(i), :].astype(BF16), bmat_ref[k])

    def s5_scan(k):
        bu, xb = bu_bufs[k], xb_bufs[k]
        a_r, a_i = abr_ref[k], abi_ref[k]
        xr, xi = s5_state[k, 0], s5_state[k, 1]
        for t in range(0, S5_SUB_T, 2):
            res_r, res_i = [], []
            for tt in (t, t + 1):
                rr = slice(tt * nb, (tt + 1) * nb)
                xr, xi = (a_r * xr - a_i * xi + bu[rr, 0:sk],
                          a_r * xi + a_i * xr + bu[rr, sk:2 * sk])
                res_r.append(xr)
                res_i.append(xi)
            r2 = slice(t * nb, (t + 2) * nb)
            xb[r2, 0:sk] = jnp.concatenate(res_r, axis=0).astype(BF16)
            xb[r2, sk:2 * sk] = jnp.concatenate(res_i, axis=0).astype(BF16)
        s5_state[k, 0] = xr
        s5_state[k, 1] = xi

    def s5_out(i, k):
        y5_buf[k, unit_rows(i), :] = _dot(xb_bufs[k][...], cmat_ref[k])

    def stage(i, k, first=False, last=False):
        s5_scan(k)
        if not last:
            s5_in(i + 1, 1 - k)
        if not first:
            s5_out(i - 1, 1 - k)
        ssd_one(i)

    s5_in(0, 0)
    for i in range(n_units):
        stage(i, i % n_half, first=i == 0, last=i == n_units - 1)
    s5_out(n_units - 1, (n_units - 1) % n_half)

    nw = ssd_w // n_groups
    for rb in range(rows // TAIL_ROWS):
        rs = slice(rb * TAIL_ROWS, (rb + 1) * TAIL_ROWS)
        y = jnp.concatenate([y_buf[j, rs, :] for j in range(xp)], axis=-1)
        y = y * _silu(z_buf[rs, :])
        parts = []
        for g in range(n_groups):
            yg = y[:, g * nw:(g + 1) * nw]
            parts.append(yg * lax.rsqrt(jnp.mean(yg * yg, axis=-1, keepdims=True) + LN_EPS))
        y_ssd = (jnp.concatenate(parts, axis=-1) * nw_row).astype(BF16)
        y5 = jnp.concatenate([y5_buf[k, rs, :] + u_buf[k, rs, :] * s5d_row[:, k * hk:(k + 1) * hk]
                              for k in range(n_half)], axis=-1)
        gl = jax.nn.gelu(y5, approximate=True)
        gate = _sigmoid(_dot(gl.astype(BF16), wg_ref[...]) + bg_row)
        y_s5 = (gl * gate).astype(BF16)
        o_ref[rs, :] = _dot(jnp.concatenate([y_ssd, y_s5], axis=-1), wo_ref[...]).astype(o_ref.dtype)


def _mixer_call(x1t, mod2, w_in, conv_w, conv_b, dt_bias, a_log, d_ssd, ssd_norm_w,
                ab_re, ab_im, bmat, cmat, s5_d, w_glu, b_glu, w_out):
    nh, rows_total, d = x1t.shape
    s = rows_total // NB
    ssd_w = ssd_norm_w.shape[0]
    n_heads = d_ssd.shape[0]
    conv_ch = conv_w.shape[1]
    s5_w = s5_d.shape[0]
    n_groups = SSD_GROUPS
    tc = SSD_CHUNK
    rows = tc * NB
    halo = (SSD_CONV - 1) * NB

    assert conv_ch == d and ssd_w + s5_w == d
    o1, o2, o3 = ssd_w, ssd_w + conv_ch, ssd_w + conv_ch + n_heads
    w_all = jnp.concatenate([w_in[:, :o2], w_in[:, o3:], w_in[:, o2:o3],
                             jnp.zeros((d, LANES - n_heads), w_in.dtype)], axis=1).astype(BF16)
    lane_pad = lambda v: jnp.pad(v, (0, LANES - n_heads))
    vecs = jnp.stack([conv_b,
                      jnp.concatenate([ssd_norm_w, s5_d]),
                      jnp.concatenate([jnp.repeat(d_ssd, SSD_HEAD_DIM), b_glu]),
                      jnp.concatenate([lane_pad(dt_bias), lane_pad(a_log),
                                       jnp.zeros((d - 2 * LANES,), F32)])])
    hexp = (jnp.arange(LANES)[:, None] == (jnp.arange(ssd_w) // SSD_HEAD_DIM)[None, :]).astype(BF16)
    hexp = jnp.concatenate([hexp] * 3, axis=0)

    n_half, hk, sk2 = bmat.shape
    kern = functools.partial(_mixer_kernel, n_heads=n_heads, n_groups=n_groups)
    xmap = lambda h, c: (h, c, 0)
    return pl.pallas_call(
        kern,
        grid=(nh, s // tc),
        in_specs=[pl.BlockSpec((None, rows, d), xmap),
                  pl.BlockSpec((None, 3, NB, d), lambda h, c: (h, 0, 0, 0)),
                  _resident(w_all.shape), _resident(conv_w.shape), _resident(vecs.shape),
                  _resident(hexp.shape),
                  _resident(ab_re.shape), _resident(ab_im.shape),
                  _resident(bmat.shape), _resident(cmat.shape), _resident(w_glu.shape),
                  _resident(w_out.shape)],
        out_specs=pl.BlockSpec((None, rows, d), xmap),
        out_shape=jax.ShapeDtypeStruct((nh, rows_total, d), BF16),
        scratch_shapes=[
            pltpu.VMEM((rows, ssd_w), F32),
            pltpu.VMEM((conv_ch // LANES, rows + halo, LANES), F32),
            pltpu.VMEM((conv_ch // LANES, rows, LANES), F32),
            pltpu.VMEM((n_half, rows, hk), F32),
            pltpu.VMEM((rows, LANES), F32),
            pltpu.VMEM((ssd_w // LANES, rows, LANES), F32),
            pltpu.VMEM((NB, ssd_w), F32),
            pltpu.VMEM((NB, n_groups, SSD_STATE, ssd_w // n_groups), F32),
            pltpu.VMEM((S5_SUB_T * NB, sk2), F32),
            pltpu.VMEM((S5_SUB_T * NB, sk2), F32),
            pltpu.VMEM((S5_SUB_T * NB, sk2), BF16),
            pltpu.VMEM((S5_SUB_T * NB, sk2), BF16),
            pltpu.VMEM((n_half, rows, hk), F32),
            pltpu.VMEM((n_half, 2, NB, sk2 // 2), F32),
        ],
        compiler_params=pltpu.CompilerParams(
            dimension_semantics=("arbitrary", "arbitrary"), vmem_limit_bytes=VMEM_LIMIT_BYTES),
        name="mixer",
    )(x1t, mod2, w_all, conv_w, vecs, hexp, ab_re, ab_im, bmat, cmat, w_glu.astype(BF16),
      w_out.astype(BF16))


def _s5_matrices(ab_re, ab_im, bb_re, bb_im, c_re, c_im):
    h, n = bb_re.shape
    g, _, p = c_re.shape
    n_half = 2
    gh = g // n_half
    eye = jnp.eye(gh, dtype=F32)

    def b_block(bb):
        v = bb.reshape(h, n_half, gh, p)
        return jnp.einsum("hkgp,gG->kghGp", v, eye).reshape(n_half, gh * h, gh * p)

    def c_block(c):
        v = c.reshape(n_half, gh, h, p)
        return jnp.einsum("kghp,gG->kgpGh", v, eye).reshape(n_half, gh * p, gh * h)

    bmat = jnp.concatenate([b_block(bb_re), b_block(bb_im)], axis=-1).astype(BF16)
    cmat = jnp.concatenate([c_block(c_re), c_block(-c_im)], axis=1).astype(BF16)
    bc = lambda v: jnp.broadcast_to(v.reshape(n_half, 1, n // n_half), (n_half, NB, n // n_half))
    return bc(ab_re), bc(ab_im), bmat, cmat


def kernel(x, c, w_ada, b_ada, ffn1_w1, ffn1_w3, ffn1_w2, ln1_g, ln1_b, w_in, conv_w, conv_b,
           dt_bias, a_log, d_ssd, ssd_norm_w, s5_a_re, s5_a_im, s5_log_dt, s5_b_re, s5_b_im,
           s5_c_re, s5_c_im, s5_d, w_glu, b_glu, w_out, ln2_g, ln2_b, ffn2_w1, ffn2_w3, ffn2_w2,
           ln3_g, ln3_b):
    bsz, s, d = x.shape
    depth = w_ada.shape[0]
    alpha = (2 * depth) ** 0.25
    assert depth == 1
    assert bsz % NB == 0 and s % SSD_CHUNK == 0 and s % FFN_TT == 0
    nh = bsz // NB
    for l in range(depth):
        mod = _ada_call(c, w_ada[l], b_ada[l]).reshape(bsz, N_MOD // 3, 3, d)
        mod = mod.reshape(nh, NB, N_MOD // 3, 3, d).transpose(2, 0, 3, 1, 4)
        ab_re, ab_im, bb_re, bb_im = _s5_prep_call(s5_a_re[l], s5_a_im[l], s5_log_dt[l],
                                                   s5_b_re[l], s5_b_im[l])
        ab_re, ab_im, bmat, cmat = _s5_matrices(ab_re, ab_im, bb_re, bb_im, s5_c_re[l], s5_c_im[l])
        ln_vecs = jnp.stack([ln1_g[l], ln1_b[l], ln2_g[l], ln2_b[l], ln3_g[l], ln3_b[l]])
        x1t = _ffn1_call(x, mod[0], ffn1_w1[l], ffn1_w3[l], ffn1_w2[l], ln_vecs, alpha)
        mix = _mixer_call(x1t, mod[1], w_in[l], conv_w[l], conv_b[l], dt_bias[l], a_log[l],
                          d_ssd[l], ssd_norm_w[l], ab_re, ab_im, bmat, cmat, s5_d[l],
                          w_glu[l], b_glu[l], w_out[l])
        x = _ffn2_call(x1t, mix, mod[1], mod[2], ln_vecs,
                       ffn2_w1[l], ffn2_w3[l], ffn2_w2[l], alpha)
    return x
```

```python
import functools

import jax
import jax.numpy as jnp
from jax import lax
from jax.experimental import pallas as pl
from jax.experimental.pallas import tpu as pltpu

F32 = jnp.float32
BF16 = jnp.bfloat16

LN_EPS = 1e-5
SUBLANES = 8
LANES = 128
MXU_TILE = 256
VMEM_LIMIT_BYTES = 56 * 1024 * 1024

SSD_HEAD_DIM = 64
SSD_GROUPS = 2
SSD_STATE = 128
SSD_CONV = 4
SSD_CHUNK = 128
N_MOD = 9

NB = SUBLANES
FFN_TT = 64
S5_SUB_T = 32
TAIL_ROWS = 256


def _dot(a, b):
    return jnp.dot(a, b, preferred_element_type=F32)


def _sigmoid(x):
    return 1.0 / (1.0 + jnp.exp(-x))


def _silu(x):
    return x * _sigmoid(x)


def _softplus(x):
    return jnp.maximum(x, 0.0) + jnp.log1p(jnp.exp(-jnp.abs(x)))


def _layer_norm(y, g, b):
    mu = jnp.mean(y, axis=-1, keepdims=True)
    yc = y - mu
    var = jnp.mean(yc * yc, axis=-1, keepdims=True)
    return yc * lax.rsqrt(var + LN_EPS) * g + b


def _resident(shape):
    nd = len(shape)
    return pl.BlockSpec(shape, lambda *_: (0,) * nd, pipeline_mode=pl.Buffered(1))


def _ada_kernel(c_ref, w_ref, b_ref, o_ref):
    cs = _silu(c_ref[...]).astype(BF16)
    o_ref[...] = _dot(cs, w_ref[...].astype(BF16)) + b_ref[...]


def _ada_call(c, w_ada, b_ada):
    bsz, d = c.shape
    n = w_ada.shape[1]
    bn = n // 8
    return pl.pallas_call(
        _ada_kernel,
        grid=(n // bn,),
        in_specs=[pl.BlockSpec((bsz, d), lambda j: (0, 0)),
                  pl.BlockSpec((d, bn), lambda j: (0, j)),
                  pl.BlockSpec((1, bn), lambda j: (0, j))],
        out_specs=pl.BlockSpec((bsz, bn), lambda j: (0, j)),
        out_shape=jax.ShapeDtypeStruct((bsz, n), F32),
        compiler_params=pltpu.CompilerParams(dimension_semantics=("arbitrary",)),
        name="ada_mod",
    )(c, w_ada, b_ada.reshape(1, n))


def _s5_prep_kernel(ar_ref, ai_ref, ldt_ref, br_ref, bi_ref,
                    abr_ref, abi_ref, bbr_ref, bbi_ref):
    ar, ai = ar_ref[...], ai_ref[...]
    dt = jnp.exp(ldt_ref[...])
    mag = jnp.exp(dt * ar)
    ab_re = mag * jnp.cos(dt * ai)
    ab_im = mag * jnp.sin(dt * ai)
    den = ar * ar + ai * ai
    nr, ni = ab_re - 1.0, ab_im
    f_re = (nr * ar + ni * ai) / den
    f_im = (ni * ar - nr * ai) / den
    br, bi = br_ref[...], bi_ref[...]
    abr_ref[...] = ab_re
    abi_ref[...] = ab_im
    bbr_ref[...] = f_re * br - f_im * bi
    bbi_ref[...] = f_re * bi + f_im * br


def _s5_prep_call(a_re, a_im, log_dt, b_re, b_im):
    g, p = a_re.shape
    h = b_re.shape[-1]
    n = g * p
    row = lambda v: v.reshape(1, n)
    to_hn = lambda v: v.reshape(n, h).T
    vec = jax.ShapeDtypeStruct((1, n), F32)
    mat = jax.ShapeDtypeStruct((h, n), F32)
    return pl.pallas_call(
        _s5_prep_kernel,
        out_shape=(vec, vec, mat, mat),
        name="s5_prep",
    )(row(a_re), row(a_im), row(jnp.repeat(log_dt, p)), to_hn(b_re), to_hn(b_im))


def _swiglu_residual(x2, h2, gate2, w1_ref, w3_ref, w2_ref, alpha):
    hb = h2.astype(BF16)
    acc = None
    for lo, hi in _ffn_chunks(w1_ref.shape[1]):
        a = _dot(hb, w1_ref[:, lo:hi])
        b = _dot(hb, w3_ref[:, lo:hi])
        gg = (_silu(a) * b).astype(BF16)
        part = _dot(gg, w2_ref[lo:hi, :])
        acc = part if acc is None else acc + part
    return alpha * x2 + (0.5 * gate2) * acc


def _delayed_epilogue(step, n_tiles, compute, finish):
    @pl.when(step == 0)
    def _():
        compute()

    @pl.when(jnp.logical_and(step > 0, step < n_tiles))
    def _():
        finish()
        compute()

    @pl.when(step == n_tiles)
    def _():
        finish()


def _ffn1_kernel(x_ref, mod_ref, w1_ref, w3_ref, w2_ref, ln_ref, o_ref, pend_ref, tmp_ref,
                 *, alpha, n_tiles):
    nb, tt, d = x_ref.shape

    def compute():
        sh, sc, gt = mod_ref[0], mod_ref[1], mod_ref[2]
        x3 = x_ref[...]
        h3 = x3 * (1.0 + sc)[:, None, :] + sh[:, None, :]
        gate3 = jnp.broadcast_to(gt[:, None, :], (nb, tt, d))
        pend_ref[...] = _swiglu_residual(x3.reshape(nb * tt, d), h3.reshape(nb * tt, d),
                                         gate3.reshape(nb * tt, d), w1_ref, w3_ref, w2_ref, alpha)

    def finish():
        out = _layer_norm(pend_ref[...], ln_ref[0:1, :], ln_ref[1:2, :])
        for j in range(d // LANES):
            for b in range(nb):
                tmp_ref[j, pl.ds(b, tt, stride=nb), :] = out[b * tt:(b + 1) * tt,
                                                             j * LANES:(j + 1) * LANES]
            o_ref[:, j * LANES:(j + 1) * LANES] = tmp_ref[j]

    _delayed_epilogue(pl.program_id(0), n_tiles, compute, finish)


def _ffn2_kernel(x_ref, m_ref, mod2_ref, mod3_ref, ln_ref,
                 w1_ref, w3_ref, w2_ref, o_ref, pend_ref, tmp_ref,
                 *, alpha, n_tiles):
    nb, tt, d = o_ref.shape

    def compute():
        x1 = x_ref[...].reshape(tt, nb, d)
        m = m_ref[...].astype(F32).reshape(tt, nb, d)
        x2 = _layer_norm((alpha * x1 + mod2_ref[2][None] * m).reshape(tt * nb, d),
                         ln_ref[2:3, :], ln_ref[3:4, :])
        sh, sc, gt = mod3_ref[0], mod3_ref[1], mod3_ref[2]
        h3 = x2.reshape(tt, nb, d) * (1.0 + sc)[None] + sh[None]
        gate3 = jnp.broadcast_to(gt[None], (tt, nb, d))
        pend_ref[...] = _swiglu_residual(x2, h3.reshape(tt * nb, d), gate3.reshape(tt * nb, d),
                                         w1_ref, w3_ref, w2_ref, alpha)

    def finish():
        out = _layer_norm(pend_ref[...], ln_ref[4:5, :], ln_ref[5:6, :])
        for j in range(d // LANES):
            tmp_ref[j] = out[:, j * LANES:(j + 1) * LANES]
            for b in range(nb):
                o_ref[b, :, j * LANES:(j + 1) * LANES] = tmp_ref[j, pl.ds(b, tt, stride=nb), :]

    _delayed_epilogue(pl.program_id(0), n_tiles, compute, finish)


def _ffn_chunks(d_ff):
    assert d_ff % MXU_TILE == 0
    cut = (d_ff // MXU_TILE + 1) // 2 * MXU_TILE
    return ((0, cut), (cut, d_ff)) if cut < d_ff else ((0, d_ff),)


def _ffn_weights(w1, w3, w2):
    return w1.astype(BF16), w3.astype(BF16), w2.astype(BF16)


def _tile_maps(n_tiles, per_half):
    def split(t):
        return t // per_half, t % per_half
    in_tile = lambda s: split(jnp.minimum(s, n_tiles - 1))
    out_tile = lambda s: split(jnp.maximum(s - 1, 0))
    return in_tile, out_tile


def _ffn1_call(x, mod1, w1, w3, w2, ln_vecs, alpha):
    bsz, s, d = x.shape
    nh = bsz // NB
    tt = FFN_TT
    per_half = s // tt
    n_tiles = nh * per_half
    in_tile, out_tile = _tile_maps(n_tiles, per_half)
    w1c, w3c, w2c = _ffn_weights(w1, w3, w2)
    return pl.pallas_call(
        functools.partial(_ffn1_kernel, alpha=alpha, n_tiles=n_tiles),
        grid=(n_tiles + 1,),
        in_specs=[pl.BlockSpec((NB, tt, d), lambda s: (*in_tile(s), 0)),
                  pl.BlockSpec((None, 3, NB, d), lambda s: (in_tile(s)[0], 0, 0, 0)),
                  _resident(w1c.shape), _resident(w3c.shape), _resident(w2c.shape),
                  _resident(ln_vecs.shape)],
        out_specs=pl.BlockSpec((None, tt * NB, d), lambda s: (*out_tile(s), 0)),
        out_shape=jax.ShapeDtypeStruct((nh, s * NB, d), F32),
        scratch_shapes=[pltpu.VMEM((tt * NB, d), F32),
                        pltpu.VMEM((d // LANES, tt * NB, LANES), F32)],
        compiler_params=pltpu.CompilerParams(
            dimension_semantics=("arbitrary",), vmem_limit_bytes=VMEM_LIMIT_BYTES),
        name="ffn1_ln1",
    )(x, mod1, w1c, w3c, w2c, ln_vecs)


def _ffn2_call(x1t, mix, mod2, mod3, ln_vecs, w1, w3, w2, alpha):
    nh, rows, d = x1t.shape
    s = rows // NB
    tt = FFN_TT
    per_half = s // tt
    n_tiles = nh * per_half
    in_tile, out_tile = _tile_maps(n_tiles, per_half)
    w1c, w3c, w2c = _ffn_weights(w1, w3, w2)
    tile_in = pl.BlockSpec((None, tt * NB, d), lambda s: (*in_tile(s), 0))
    mod_in = pl.BlockSpec((None, 3, NB, d), lambda s: (in_tile(s)[0], 0, 0, 0))
    return pl.pallas_call(
        functools.partial(_ffn2_kernel, alpha=alpha, n_tiles=n_tiles),
        grid=(n_tiles + 1,),
        in_specs=[tile_in, tile_in, mod_in, mod_in,
                  _resident(ln_vecs.shape),
                  _resident(w1c.shape), _resident(w3c.shape), _resident(w2c.shape)],
        out_specs=pl.BlockSpec((NB, tt, d), lambda s: (*out_tile(s), 0)),
        out_shape=jax.ShapeDtypeStruct((nh * NB, s, d), F32),
        scratch_shapes=[pltpu.VMEM((tt * NB, d), F32),
                        pltpu.VMEM((d // LANES, tt * NB, LANES), F32)],
        compiler_params=pltpu.CompilerParams(
            dimension_semantics=("arbitrary",), vmem_limit_bytes=VMEM_LIMIT_BYTES),
        name="ln2_ffn2_ln3",
    )(x1t, mix, mod2, mod3, ln_vecs, w1c, w3c, w2c)


def _mixer_kernel(x_ref, mod_ref, win_ref, cw_ref, vec_ref, hexp3_ref,
                  abr_ref, abi_ref, bmat_ref, cmat_ref, wg_ref, wo_ref,
                  o_ref,
                  z_buf, xbc_buf, act_buf, u_buf, acs_buf, y_buf, cdec_buf,
                  hstate, bu0_buf, bu1_buf, xb0_buf, xb1_buf, y5_buf, s5_state,
                  *, n_heads, n_groups):
    rows, d = x_ref.shape
    nb = NB
    tc = rows // nb
    halo = (SSD_CONV - 1) * nb
    ssd_w = z_buf.shape[1]
    conv_ch = xbc_buf.shape[0] * LANES
    xp = ssd_w // LANES
    assert SSD_STATE == LANES
    hpg = n_heads // n_groups
    gw = hpg * SSD_HEAD_DIM
    n_half, hk = bmat_ref.shape[0], bmat_ref.shape[1]
    sk = bmat_ref.shape[2] // 2
    s5_w = n_half * hk
    c_idx = pl.program_id(1)
    o_x, o_u, o_dt = ssd_w, ssd_w + conv_ch, ssd_w + conv_ch + s5_w
    cb_row = vec_ref[0:1, :]
    nw_row, s5d_row = vec_ref[1:2, 0:ssd_w], vec_ref[1:2, ssd_w:ssd_w + s5_w]
    dfull_row, bg_row = vec_ref[2:3, 0:ssd_w], vec_ref[2:3, ssd_w:ssd_w + s5_w]
    dtb_row, alog_row = vec_ref[3:4, 0:LANES], vec_ref[3:4, LANES:2 * LANES]

    @pl.when(c_idx == 0)
    def _():
        xbc_buf[:, 0:halo, :] = jnp.zeros((conv_ch // LANES, halo, LANES), F32)
        hstate[...] = jnp.zeros(hstate.shape, F32)
        s5_state[...] = jnp.zeros(s5_state.shape, F32)

    sh, sc = mod_ref[0], mod_ref[1]
    hb = (x_ref[...].reshape(tc, nb, d) * (1.0 + sc)[None] + sh[None]).reshape(rows, d).astype(BF16)
    xbc = _dot(hb, win_ref[:, o_x:o_u])
    for j in range(conv_ch // LANES):
        xbc_buf[j, halo:halo + rows, :] = xbc[:, j * LANES:(j + 1) * LANES]
    dt_raw = _dot(hb, win_ref[:, o_dt:o_dt + LANES])
    for k in range(n_half):
        u_buf[k] = _dot(hb, win_ref[:, o_u + k * hk:o_u + (k + 1) * hk])
    z_buf[...] = _dot(hb, win_ref[:, 0:o_x])

    for j in range(conv_ch // LANES):
        cs = slice(j * LANES, (j + 1) * LANES)
        conv = cb_row[:, cs]
        for k in range(SSD_CONV):
            conv = conv + cw_ref[k:k + 1, cs] * xbc_buf[j, k * nb:k * nb + rows, :]
        act_buf[j] = _silu(conv)
        xbc_buf[j, 0:halo, :] = xbc_buf[j, rows:rows + halo, :]

    dt = _softplus(dt_raw + dtb_row)
    acs = dt * (-jnp.exp(alog_row))
    shift = nb
    while shift < rows:
        acs = acs + jnp.concatenate([jnp.zeros((shift, LANES), F32), acs[:rows - shift]], axis=0)
        shift *= 2
    acs_buf[...] = acs
    hi = acs.astype(BF16)
    r1 = acs - hi.astype(F32)
    mid = r1.astype(BF16)
    lo = (r1 - mid.astype(F32)).astype(BF16)
    acs_full = _dot(jnp.concatenate([hi, mid, lo], axis=-1), hexp3_ref[...])
    dt_full = _dot(dt.astype(BF16), hexp3_ref[0:LANES, :])
    last = acs_full[rows - nb:rows, :]
    cdec_buf[...] = jnp.exp(last)
    dec = jnp.exp(last[None] - acs_full.reshape(tc, nb, ssd_w)).reshape(rows, ssd_w)
    eac = jnp.exp(acs_full)
    for j in range(xp):
        cs = slice(j * LANES, (j + 1) * LANES)
        xs = act_buf[j]
        xin = xs * dt_full[:, cs]
        y_buf[j] = dfull_row[:, cs] * xs
        act_buf[j] = xin
        xbc_buf[j, halo:halo + rows, :] = xin * dec[:, cs]
        xbc_buf[xp + j, halo:halo + rows, :] = eac[:, cs]

    causal = (lax.broadcasted_iota(jnp.int32, (tc, tc), 0)
              >= lax.broadcasted_iota(jnp.int32, (tc, tc), 1))
    gp = gw // LANES
    lane_head = lax.broadcasted_iota(jnp.int32, (tc, gw), 1) // SSD_HEAD_DIM

    def ssd_group(b, g, rsel, acs_b, acs_t, cdec):
        gather = lambda buf, p0, rs: jnp.concatenate(
            [buf[p0 + g * gp + jj, rs, :] for jj in range(gp)], axis=-1)
        rsel_h = pl.ds(halo + b, tc, stride=nb)
        xin_g = gather(act_buf, 0, rsel)
        xdec_g, eacs_g = gather(xbc_buf, 0, rsel_h), gather(xbc_buf, xp, rsel_h)
        bm = act_buf[xp + g, rsel, :]
        cm_b = act_buf[xp + n_groups + g, rsel, :].astype(BF16)
        cbm = lax.dot_general(cm_b, bm.astype(BF16), (((1,), (1,)), ((), ())),
                              preferred_element_type=F32)
        h_prev = hstate[b, g]
        y_off = _dot(cm_b, h_prev.astype(BF16)) * eacs_g
        mms, xblk = [], []
        for z in range(hpg):
            hd = g * hpg + z
            seg = acs_b[:, hd:hd + 1] - acs_t[hd:hd + 1, :]
            lm = jnp.where(causal, jnp.exp(seg), 0.0)
            mms.append((cbm * lm).astype(BF16))
            xblk.append(jnp.where(lane_head == z, xin_g, 0.0).astype(BF16))
        y_g = _dot(jnp.concatenate(mms, axis=-1), jnp.concatenate(xblk, axis=0)) + y_off
        for jj in range(gp):
            y_buf[g * gp + jj, rsel, :] = (y_buf[g * gp + jj, rsel, :]
                                           + y_g[:, jj * LANES:(jj + 1) * LANES])
        st = _dot(bm.T.astype(BF16), xdec_g.astype(BF16))
        hstate[b, g] = cdec[:, g * gw:(g + 1) * gw] * h_prev + st

    def ssd_one(b):
        rsel = pl.ds(b, tc, stride=nb)
        acs_b = acs_buf[rsel, :]
        acs_t = acs_b.T
        cdec = cdec_buf[pl.ds(b, 1), :]
        for g in range(n_groups):
            ssd_group(b, g, rsel, acs_b, acs_t, cdec)

    sub_rows = S5_SUB_T * nb
    n_units = (tc // S5_SUB_T) * n_half
    assert n_units == nb and n_half == 2
    bu_bufs, xb_bufs = (bu0_buf, bu1_buf), (xb0_buf, xb1_buf)

    def unit_rows(i):
        start = (i // n_half) * sub_rows
        return pl.ds(start if isinstance(i, int) else pl.multiple_of(start, sub_rows), sub_rows)

    def s5_in(i, k):
        bu_bufs[k][...] = _dot(u_buf[k, unit_rows(i), :].astype(BF16), bmat_ref[k])

    def s5_scan(k):
        bu, xb = bu_bufs[k], xb_bufs[k]
        a_r, a_i = abr_ref[k], abi_ref[k]
        xr, xi = s5_state[k, 0], s5_state[k, 1]
        for t in range(0, S5_SUB_T, 2):
            res_r, res_i = [], []
            for tt in (t, t + 1):
                rr = slice(tt * nb, (tt + 1) * nb)
                xr, xi = (a_r * xr - a_i * xi + bu[rr, 0:sk],
                          a_r * xi + a_i * xr + bu[rr, sk:2 * sk])
                res_r.append(xr)
                res_i.append(xi)
            r2 = slice(t * nb, (t + 2) * nb)
            xb[r2, 0:sk] = jnp.concatenate(res_r, axis=0).astype(BF16)
            xb[r2, sk:2 * sk] = jnp.concatenate(res_i, axis=0).astype(BF16)
        s5_state[k, 0] = xr
        s5_state[k, 1] = xi

    def s5_out(i, k):
        y5_buf[k, unit_rows(i), :] = _dot(xb_bufs[k][...], cmat_ref[k])

    def stage(i, k, first=False, last=False):
        s5_scan(k)
        if not last:
            s5_in(i + 1, 1 - k)
        if not first:
            s5_out(i - 1, 1 - k)
        ssd_one(i)

    s5_in(0, 0)
    for i in range(n_units):
        stage(i, i % n_half, first=i == 0, last=i == n_units - 1)
    s5_out(n_units - 1, (n_units - 1) % n_half)

    nw = ssd_w // n_groups
    n_blocks = rows // TAIL_ROWS
    pending = None
    for rb in range(n_blocks + 1):
        if rb == n_blocks:
            o_ref[pending[0], :] = _dot(pending[1], wo_ref[...]).astype(o_ref.dtype)
            break
        rs = slice(rb * TAIL_ROWS, (rb + 1) * TAIL_ROWS)
        y = jnp.concatenate([y_buf[j, rs, :] for j in range(xp)], axis=-1)
        y = y * _silu(z_buf[rs, :])
        parts = []
        for g in range(n_groups):
            yg = y[:, g * nw:(g + 1) * nw]
            parts.append(yg * lax.rsqrt(jnp.mean(yg * yg, axis=-1, keepdims=True) + LN_EPS))
        y_ssd = (jnp.concatenate(parts, axis=-1) * nw_row).astype(BF16)
        y5 = jnp.concatenate([y5_buf[k, rs, :] + u_buf[k, rs, :] * s5d_row[:, k * hk:(k + 1) * hk]
                              for k in range(n_half)], axis=-1)
        gl = jax.nn.gelu(y5, approximate=True)
        gate = _sigmoid(_dot(gl.astype(BF16), wg_ref[...]) + bg_row)
        y_s5 = (gl * gate).astype(BF16)
        if pending is not None:
            o_ref[pending[0], :] = _dot(pending[1], wo_ref[...]).astype(o_ref.dtype)
        pending = (rs, jnp.concatenate([y_ssd, y_s5], axis=-1))


def _mixer_call(x1t, mod2, w_in, conv_w, conv_b, dt_bias, a_log, d_ssd, ssd_norm_w,
                ab_re, ab_im, bmat, cmat, s5_d, w_glu, b_glu, w_out):
    nh, rows_total, d = x1t.shape
    s = rows_total // NB
    ssd_w = ssd_norm_w.shape[0]
    n_heads = d_ssd.shape[0]
    conv_ch = conv_w.shape[1]
    s5_w = s5_d.shape[0]
    n_groups = SSD_GROUPS
    tc = SSD_CHUNK
    rows = tc * NB
    halo = (SSD_CONV - 1) * NB

    assert conv_ch == d and ssd_w + s5_w == d
    o1, o2, o3 = ssd_w, ssd_w + conv_ch, ssd_w + conv_ch + n_heads
    w_all = jnp.concatenate([w_in[:, :o2], w_in[:, o3:], w_in[:, o2:o3],
                             jnp.zeros((d, LANES - n_heads), w_in.dtype)], axis=1).astype(BF16)
    lane_pad = lambda v: jnp.pad(v, (0, LANES - n_heads))
    vecs = jnp.stack([conv_b,
                      jnp.concatenate([ssd_norm_w, s5_d]),
                      jnp.concatenate([jnp.repeat(d_ssd, SSD_HEAD_DIM), b_glu]),
                      jnp.concatenate([lane_pad(dt_bias), lane_pad(a_log),
                                       jnp.zeros((d - 2 * LANES,), F32)])])
    hexp = (jnp.arange(LANES)[:, None] == (jnp.arange(ssd_w) // SSD_HEAD_DIM)[None, :]).astype(BF16)
    hexp = jnp.concatenate([hexp] * 3, axis=0)

    n_half, hk, sk2 = bmat.shape
    kern = functools.partial(_mixer_kernel, n_heads=n_heads, n_groups=n_groups)
    xmap = lambda h, c: (h, c, 0)
    return pl.pallas_call(
        kern,
        grid=(nh, s // tc),
        in_specs=[pl.BlockSpec((None, rows, d), xmap),
                  pl.BlockSpec((None, 3, NB, d), lambda h, c: (h, 0, 0, 0)),
                  _resident(w_all.shape), _resident(conv_w.shape), _resident(vecs.shape),
                  _resident(hexp.shape),
                  _resident(ab_re.shape), _resident(ab_im.shape),
                  _resident(bmat.shape), _resident(cmat.shape), _resident(w_glu.shape),
                  _resident(w_out.shape)],
        out_specs=pl.BlockSpec((None, rows, d), xmap),
        out_shape=jax.ShapeDtypeStruct((nh, rows_total, d), BF16),
        scratch_shapes=[
            pltpu.VMEM((rows, ssd_w), F32),
            pltpu.VMEM((conv_ch // LANES, rows + halo, LANES), F32),
            pltpu.VMEM((conv_ch // LANES, rows, LANES), F32),
            pltpu.VMEM((n_half, rows, hk), F32),
            pltpu.VMEM((rows, LANES), F32),
            pltpu.VMEM((ssd_w // LANES, rows, LANES), F32),
            pltpu.VMEM((NB, ssd_w), F32),
            pltpu.VMEM((NB, n_groups, SSD_STATE, ssd_w // n_groups), F32),
            pltpu.VMEM((S5_SUB_T * NB, sk2), F32),
            pltpu.VMEM((S5_SUB_T * NB, sk2), F32),
            pltpu.VMEM((S5_SUB_T * NB, sk2), BF16),
            pltpu.VMEM((S5_SUB_T * NB, sk2), BF16),
            pltpu.VMEM((n_half, rows, hk), F32),
            pltpu.VMEM((n_half, 2, NB, sk2 // 2), F32),
        ],
        compiler_params=pltpu.CompilerParams(
            dimension_semantics=("arbitrary", "arbitrary"), vmem_limit_bytes=VMEM_LIMIT_BYTES),
        name="mixer",
    )(x1t, mod2, w_all, conv_w, vecs, hexp, ab_re, ab_im, bmat, cmat, w_glu.astype(BF16),
      w_out.astype(BF16))


def _s5_matrices(ab_re, ab_im, bb_re, bb_im, c_re, c_im):
    h, n = bb_re.shape
    g, _, p = c_re.shape
    n_half = 2
    gh = g // n_half
    eye = jnp.eye(gh, dtype=F32)

    def b_block(bb):
        v = bb.reshape(h, n_half, gh, p)
        return jnp.einsum("hkgp,gG->kghGp", v, eye).reshape(n_half, gh * h, gh * p)

    def c_block(c):
        v = c.reshape(n_half, gh, h, p)
        return jnp.einsum("kghp,gG->kgpGh", v, eye).reshape(n_half, gh * p, gh * h)

    bmat = jnp.concatenate([b_block(bb_re), b_block(bb_im)], axis=-1).astype(BF16)
    cmat = jnp.concatenate([c_block(c_re), c_block(-c_im)], axis=1).astype(BF16)
    bc = lambda v: jnp.broadcast_to(v.reshape(n_half, 1, n // n_half), (n_half, NB, n // n_half))
    return bc(ab_re), bc(ab_im), bmat, cmat


def kernel(x, c, w_ada, b_ada, ffn1_w1, ffn1_w3, ffn1_w2, ln1_g, ln1_b, w_in, conv_w, conv_b,
           dt_bias, a_log, d_ssd, ssd_norm_w, s5_a_re, s5_a_im, s5_log_dt, s5_b_re, s5_b_im,
           s5_c_re, s5_c_im, s5_d, w_glu, b_glu, w_out, ln2_g, ln2_b, ffn2_w1, ffn2_w3, ffn2_w2,
           ln3_g, ln3_b):
    bsz, s, d = x.shape
    depth = w_ada.shape[0]
    alpha = (2 * depth) ** 0.25
    assert depth == 1
    assert bsz % NB == 0 and s % SSD_CHUNK == 0 and s % FFN_TT == 0
    nh = bsz // NB
    for l in range(depth):
        mod = _ada_call(c, w_ada[l], b_ada[l]).reshape(bsz, N_MOD // 3, 3, d)
        mod = mod.reshape(nh, NB, N_MOD // 3, 3, d).transpose(2, 0, 3, 1, 4)
        ab_re, ab_im, bb_re, bb_im = _s5_prep_call(s5_a_re[l], s5_a_im[l], s5_log_dt[l],
                                                   s5_b_re[l], s5_b_im[l])
        ab_re, ab_im, bmat, cmat = _s5_matrices(ab_re, ab_im, bb_re, bb_im, s5_c_re[l], s5_c_im[l])
        ln_vecs = jnp.stack([ln1_g[l], ln1_b[l], ln2_g[l], ln2_b[l], ln3_g[l], ln3_b[l]])
        x1t = _ffn1_call(x, mod[0], ffn1_w1[l], ffn1_w3[l], ffn1_w2[l], ln_vecs, alpha)
        mix = _mixer_call(x1t, mod[1], w_in[l], conv_w[l], conv_b[l], dt_bias[l], a_log[l],
                          d_ssd[l], ssd_norm_w[l], ab_re, ab_im, bmat, cmat, s5_d[l],
                          w_glu[l], b_glu[l], w_out[l])
        x = _ffn2_call(x1t, mix, mod[1], mod[2], ln_vecs,
                       ffn2_w1[l], ffn2_w3[l], ffn2_w2[l], alpha)
    return x
```
